```python
import jax, jax.numpy as jnp
from jax import lax
import numpy as np

D_MODEL = 1024
BATCH = 2
SEQ = 8192
DEPTH = 1
DEC_BATCH = 8
DEC_SEQ = 32
PAST_LEN = 4096

CHUNK = 64
D_MIX = D_MODEL
D_CONV = D_MIX // 2
CONV_W = 3
N_HEADS = 8
HEAD_DIM = (D_MIX - D_CONV) // N_HEADS
N_KV_HEADS = 2
GROUP = N_HEADS // N_KV_HEADS
IDX_HEADS = 8
IDX_DIM = 64
TOPK_MAX = 256
Q_BLOCK = 128
N_MEM = 256
MEM_HEADS = 4
MEM_HEAD_DIM = D_MODEL // MEM_HEADS
D_FF = 2816
ALPHA = (2.0 * DEPTH) ** 0.25
BETA = (8.0 * DEPTH) ** -0.25
LN_EPS = 1e-5
MIX_SPLITS = (D_CONV, D_CONV, D_CONV, N_HEADS * HEAD_DIM, N_KV_HEADS * HEAD_DIM,
              N_KV_HEADS * HEAD_DIM, IDX_HEADS * IDX_DIM, IDX_DIM, IDX_HEADS)
D_MIX_IN = sum(MIX_SPLITS)

kernel_name = "hybrid_conv_dsa_streaming_step"


def _layer_norm(x, g, b):
    xf = x.astype(jnp.float32)
    mu = xf.mean(-1, keepdims=True)
    var = jnp.square(xf - mu).mean(-1, keepdims=True)
    y = (xf - mu) * lax.rsqrt(var + LN_EPS) * g.astype(jnp.float32) + b.astype(jnp.float32)
    return y.astype(x.dtype)


def _swiglu(x, w_gate, w_up, w_down):
    return (jax.nn.silu(x @ w_gate) * (x @ w_up)) @ w_down


def _dsa_attention(q, k, v, qi, ki, wi, q_pos, k_pos, top_k):
    b, t = q.shape[:2]
    qb = min(Q_BLOCK, t)
    nb = t // qb
    k_chunk = k_pos // CHUNK
    gather = jax.vmap(lambda rows, ids: rows[ids])

    def blocks(a):
        return a.reshape((b, nb, qb) + a.shape[2:]).swapaxes(0, 1)

    def one_block(args):
        q_b, qi_b, wi_b, qpos_b = args
        q_chunk = qpos_b // CHUNK
        logits = jnp.einsum('bqhd,bsd->bqsh', qi_b, ki)
        score = jnp.einsum('bqsh,bqh->bqs', jax.nn.relu(logits), wi_b).astype(jnp.float32)
        admissible = k_chunk[None, :] <= q_chunk[:, None]
        score = jnp.where(admissible[None], score, -jnp.inf)
        _, idx = lax.top_k(score, top_k)
        k_sel = gather(k, idx)
        v_sel = gather(v, idx)
        valid = k_chunk[idx] <= q_chunk[None, :, None]
        s = jnp.einsum('bqhgd,bqkhd->bqhgk', q_b, k_sel).astype(jnp.float32) * (HEAD_DIM ** -0.5)
        s = jnp.where(valid[:, :, None, None, :], s, -jnp.inf)
        p = jax.nn.softmax(s, axis=-1).astype(v.dtype)
        return jnp.einsum('bqhgk,bqkhd->bqhgd', p, v_sel)

    out = lax.map(one_block, (blocks(q), blocks(qi), blocks(wi), q_pos.reshape(nb, qb)))
    return out.swapaxes(0, 1).reshape(q.shape)


def _mem_attention(x, mem_k, mem_v, w_q, w_o):
    b, t, _ = x.shape
    q = (x @ w_q).reshape(b, t, MEM_HEADS, MEM_HEAD_DIM)
    s = jnp.einsum('bthd,bmhd->bhtm', q, mem_k).astype(jnp.float32) * (MEM_HEAD_DIM ** -0.5)
    p = jax.nn.softmax(s, axis=-1).astype(mem_v.dtype)
    o = jnp.einsum('bhtm,bmhd->bthd', p, mem_v).reshape(b, t, MEM_HEADS * MEM_HEAD_DIM)
    return o @ w_o


def _layer(x, conv_prev, k_past, v_past, ik_past, mem_k, mem_v, pos0, top_k, p):
    b, t, _ = x.shape
    x = _layer_norm(ALPHA * x + 0.5 * _swiglu(x, p['ffn1_gate'], p['ffn1_up'], p['ffn1_down']),
                    p['ln1_g'], p['ln1_b'])
    proj = x @ p['w_mix_in']
    h, gate_b, gate_c, q, k, v, qi, ki, wi = jnp.split(
        proj, np.cumsum(MIX_SPLITS)[:-1].tolist(), axis=-1)
    u = gate_c * h
    u_pad = jnp.concatenate([conv_prev, u], axis=1)
    w = p['conv_w']
    conv = sum(w[j] * u_pad[:, j:j + t] for j in range(CONV_W))
    y_conv = gate_b * conv
    new_conv = u_pad[:, -(CONV_W - 1):]
    q = q.reshape(b, t, N_KV_HEADS, GROUP, HEAD_DIM)
    k = k.reshape(b, t, N_KV_HEADS, HEAD_DIM)
    v = v.reshape(b, t, N_KV_HEADS, HEAD_DIM)
    qi = qi.reshape(b, t, IDX_HEADS, IDX_DIM)
    wi = wi * ((IDX_HEADS * IDX_DIM) ** -0.5)
    k_all = jnp.concatenate([k_past, k], axis=1)
    v_all = jnp.concatenate([v_past, v], axis=1)
    ki_all = jnp.concatenate([ik_past, ki], axis=1)
    q_pos = pos0 + jnp.arange(t)
    k_pos = jnp.arange(k_all.shape[1])
    y_attn = _dsa_attention(q, k_all, v_all, qi, ki_all, wi, q_pos, k_pos, top_k)
    y_attn = y_attn.reshape(b, t, N_HEADS * HEAD_DIM)
    mix = jnp.concatenate([y_conv, y_attn], axis=-1) @ p['w_mix_out']
    x = _layer_norm(ALPHA * x + mix, p['ln2_g'], p['ln2_b'])
    x = _layer_norm(ALPHA * x + _mem_attention(x, mem_k, mem_v, p['w_mem_q'], p['w_mem_o']),
                    p['ln3_g'], p['ln3_b'])
    x = _layer_norm(ALPHA * x + 0.5 * _swiglu(x, p['ffn2_gate'], p['ffn2_up'], p['ffn2_down']),
                    p['ln4_g'], p['ln4_b'])
    return x, new_conv, k, v, ki


def setup_inputs(seed: int = 0) -> dict:
    key = jax.random.key(seed)
    ks = iter(jax.random.split(key, 40))
    f32 = jnp.float32

    def nrm(shape, scale=1.0):
        return jax.random.normal(next(ks), shape, f32) * scale

    def gain():
        return 1.0 + nrm((DEPTH, D_MODEL), 0.05)

    def bias():
        return nrm((DEPTH, D_MODEL), 0.02)

    return {
        "x_prompt": nrm((BATCH, SEQ, D_MODEL)),
        "x_sample": nrm((DEC_BATCH, DEC_SEQ, D_MODEL)),
        "cache_conv": nrm((DEPTH, DEC_BATCH, CONV_W - 1, D_CONV)),
        "cache_k": nrm((DEPTH, DEC_BATCH, PAST_LEN, N_KV_HEADS, HEAD_DIM)),
        "cache_v": nrm((DEPTH, DEC_BATCH, PAST_LEN, N_KV_HEADS, HEAD_DIM)),
        "cache_idx_k": nrm((DEPTH, DEC_BATCH, PAST_LEN, IDX_DIM)),
        "cache_mem_k": nrm((DEPTH, DEC_BATCH, N_MEM, MEM_HEADS, MEM_HEAD_DIM)),
        "cache_mem_v": nrm((DEPTH, DEC_BATCH, N_MEM, MEM_HEADS, MEM_HEAD_DIM)),
        "mem_prompt": nrm((BATCH, N_MEM, D_MODEL)),
        "ffn1_gate": nrm((DEPTH, D_MODEL, D_FF), D_MODEL ** -0.5),
        "ffn1_up": nrm((DEPTH, D_MODEL, D_FF), D_MODEL ** -0.5),
        "ffn1_down": nrm((DEPTH, D_FF, D_MODEL), BETA * D_FF ** -0.5),
        "ln1_g": gain(), "ln1_b": bias(),
        "w_mix_in": nrm((DEPTH, D_MODEL, D_MIX_IN), D_MODEL ** -0.5),
        "conv_w": nrm((DEPTH, CONV_W, D_CONV), CONV_W ** -0.5),
        "w_mix_out": nrm((DEPTH, D_MIX, D_MODEL), BETA * D_MIX ** -0.5),
        "ln2_g": gain(), "ln2_b": bias(),
        "w_mem_q": nrm((DEPTH, D_MODEL, MEM_HEADS * MEM_HEAD_DIM), D_MODEL ** -0.5),
        "w_mem_k": nrm((DEPTH, D_MODEL, MEM_HEADS * MEM_HEAD_DIM), D_MODEL ** -0.5),
        "w_mem_v": nrm((DEPTH, D_MODEL, MEM_HEADS * MEM_HEAD_DIM), D_MODEL ** -0.5),
        "w_mem_o": nrm((DEPTH, MEM_HEADS * MEM_HEAD_DIM, D_MODEL), BETA * D_MODEL ** -0.5),
        "ln3_g": gain(), "ln3_b": bias(),
        "ffn2_gate": nrm((DEPTH, D_MODEL, D_FF), D_MODEL ** -0.5),
        "ffn2_up": nrm((DEPTH, D_MODEL, D_FF), D_MODEL ** -0.5),
        "ffn2_down": nrm((DEPTH, D_FF, D_MODEL), BETA * D_FF ** -0.5),
        "ln4_g": gain(), "ln4_b": bias(),
    }


def reference(x_prompt, x_sample, cache_conv, cache_k, cache_v, cache_idx_k, cache_mem_k, cache_mem_v,
              mem_prompt, ffn1_gate, ffn1_up, ffn1_down, ln1_g, ln1_b, w_mix_in, conv_w, w_mix_out,
              ln2_g, ln2_b, w_mem_q, w_mem_k, w_mem_v, w_mem_o, ln3_g, ln3_b,
              ffn2_gate, ffn2_up, ffn2_down, ln4_g, ln4_b):
    b_p, t_p, _ = x_prompt.shape
    top_k_prompt = min(TOPK_MAX, t_p // 4)
    top_k_sample = min(TOPK_MAX, (cache_k.shape[2] + x_sample.shape[1]) // 4)
    past = cache_k.shape[2]
    y_p, y_s = x_prompt, x_sample
    conv_p, k_p, v_p, ik_p, mk_p, mv_p = [], [], [], [], [], []
    conv_s, k_s, v_s, ik_s = [], [], [], []
    for l in range(DEPTH):
        p = dict(ffn1_gate=ffn1_gate[l], ffn1_up=ffn1_up[l], ffn1_down=ffn1_down[l],
                 ln1_g=ln1_g[l], ln1_b=ln1_b[l], w_mix_in=w_mix_in[l], conv_w=conv_w[l],
                 w_mix_out=w_mix_out[l], ln2_g=ln2_g[l], ln2_b=ln2_b[l],
                 w_mem_q=w_mem_q[l], w_mem_o=w_mem_o[l], ln3_g=ln3_g[l], ln3_b=ln3_b[l],
                 ffn2_gate=ffn2_gate[l], ffn2_up=ffn2_up[l], ffn2_down=ffn2_down[l],
                 ln4_g=ln4_g[l], ln4_b=ln4_b[l])
        mem_k = (mem_prompt @ w_mem_k[l]).reshape(b_p, N_MEM, MEM_HEADS, MEM_HEAD_DIM)
        mem_v = (mem_prompt @ w_mem_v[l]).reshape(b_p, N_MEM, MEM_HEADS, MEM_HEAD_DIM)
        dt = x_prompt.dtype
        y_p, c_new, k_new, v_new, ik_new = _layer(
            y_p, jnp.zeros((b_p, CONV_W - 1, D_CONV), dt),
            jnp.zeros((b_p, 0, N_KV_HEADS, HEAD_DIM), dt), jnp.zeros((b_p, 0, N_KV_HEADS, HEAD_DIM), dt),
            jnp.zeros((b_p, 0, IDX_DIM), dt), mem_k, mem_v, 0, top_k_prompt, p)
        conv_p.append(c_new); k_p.append(k_new); v_p.append(v_new); ik_p.append(ik_new)
        mk_p.append(mem_k); mv_p.append(mem_v)
        y_s, c_new, k_new, v_new, ik_new = _layer(
            y_s, cache_conv[l], cache_k[l], cache_v[l], cache_idx_k[l],
            cache_mem_k[l], cache_mem_v[l], past, top_k_sample, p)
        conv_s.append(c_new); k_s.append(k_new); v_s.append(v_new); ik_s.append(ik_new)
    return (y_p, y_s,
            jnp.stack(conv_p), jnp.stack(k_p), jnp.stack(v_p), jnp.stack(ik_p),
            jnp.stack(mk_p), jnp.stack(mv_p),
            jnp.stack(conv_s), jnp.stack(k_s), jnp.stack(v_s), jnp.stack(ik_s))
```

```python
import functools

import jax
import jax.numpy as jnp
from jax import lax
from jax.experimental import pallas as pl
from jax.experimental.pallas import tpu as pltpu

F32 = jnp.float32
BF16 = jnp.bfloat16
I32 = jnp.int32

CHUNK = 64
N_HEADS = 8
HEAD_DIM = 64
N_KV_HEADS = 2
GROUP = N_HEADS // N_KV_HEADS
IDX_HEADS = 8
IDX_DIM = 64
TOPK_MAX = 256
MEM_HEADS = 4
LN_EPS = 1e-5

LANES = 128
VMEM_LIMIT = 56 * 1024 * 1024
NEG_BIG = -1e30
F32_LOWEST = -3.4028234663852886e38
KEY_NEG_INF = -2139095041
KEY_POS_INF = 2139095040


def _dot(a, b):
    return jnp.dot(a, b, preferred_element_type=F32)


def _dot_nt(a, b):
    return lax.dot_general(a, b, (((1,), (1,)), ((), ())), preferred_element_type=F32)


def _layer_norm(z, g, b):
    mu = jnp.mean(z, axis=-1, keepdims=True)
    d = z - mu
    var = jnp.mean(d * d, axis=-1, keepdims=True)
    return d * lax.rsqrt(var + LN_EPS) * g + b


def _swiglu_into(acc_ref, xb, wg_ref, wu_ref, wd_ref, f_chunk):
    d_ff = wg_ref.shape[1]
    for c in range(d_ff // f_chunk):
        sl = slice(c * f_chunk, (c + 1) * f_chunk)
        g = _dot(xb, wg_ref[:, sl])
        u = _dot(xb, wu_ref[:, sl])
        h = (g * (1.0 / (1.0 + jnp.exp(-g)))) * u
        part = _dot(h.astype(BF16), wd_ref[sl, :])
        if c == 0:
            acc_ref[...] = part
        else:
            acc_ref[...] += part


def _tok_a_kernel(x_ref, cprev_ref, wg_ref, wu_ref, wd_ref, g_ref, b_ref, win_ref, cw_ref,
                  x1_ref, yconv_ref, q_ref, k_ref, v_ref, qi_ref, tail_ref, ulast_ref,
                  acc_ref, ubuf_ref, *, alpha, tiles_per_seg, f_chunk, d_conv):
    i = pl.program_id(0)
    tm = x_ref.shape[0]
    x = x_ref[...]
    _swiglu_into(acc_ref, x.astype(BF16), wg_ref, wu_ref, wd_ref, f_chunk)
    x1 = _layer_norm(alpha * x + 0.5 * acc_ref[...], g_ref[...], b_ref[...])
    x1_ref[...] = x1
    x1b = x1.astype(BF16)

    c0 = 0
    h = _dot(x1b, win_ref[:, c0:c0 + d_conv]); c0 += d_conv
    gate_b = _dot(x1b, win_ref[:, c0:c0 + d_conv]); c0 += d_conv
    gate_c = _dot(x1b, win_ref[:, c0:c0 + d_conv]); c0 += d_conv
    nq = N_HEADS * HEAD_DIM
    q_ref[...] = (_dot(x1b, win_ref[:, c0:c0 + nq]) * (HEAD_DIM ** -0.5)).astype(BF16); c0 += nq
    nkv = N_KV_HEADS * HEAD_DIM
    k_ref[...] = _dot(x1b, win_ref[:, c0:c0 + nkv]); c0 += nkv
    v_ref[...] = _dot(x1b, win_ref[:, c0:c0 + nkv]); c0 += nkv
    nqi = IDX_HEADS * IDX_DIM
    qi_ref[...] = _dot(x1b, win_ref[:, c0:c0 + nqi]).astype(BF16); c0 += nqi
    tail_ref[...] = _dot(x1b, win_ref[:, c0:c0 + LANES])

    u = gate_c * h

    @pl.when(i % tiles_per_seg == 0)
    def _():
        ubuf_ref[0:8, :] = cprev_ref[0]

    ubuf_ref[8:tm + 8, :] = u
    um1 = ubuf_ref[7:tm + 7, :]
    um2 = ubuf_ref[6:tm + 6, :]
    cw = cw_ref[...]
    conv = cw[0:1, :] * um2 + cw[1:2, :] * um1 + cw[2:3, :] * u
    yconv_ref[...] = gate_b * conv
    last8 = ubuf_ref[tm:tm + 8, :]
    ulast_ref[0] = last8
    ubuf_ref[0:8, :] = last8


def _const_spec(shape):
    nd = len(shape)
    return pl.BlockSpec(shape, lambda *_: (0,) * nd, pipeline_mode=pl.Buffered(1))


def _tok_a(x, cprev, wg, wu, wd, g, b, win, cw, *, alpha, tm, seg_len):
    n, d = x.shape
    d_ff = wg.shape[1]
    d_conv = cw.shape[1]
    n_tiles = n // tm
    row = lambda w: pl.BlockSpec((tm, w), lambda i: (i, 0))
    nq, nkv, nqi = N_HEADS * HEAD_DIM, N_KV_HEADS * HEAD_DIM, IDX_HEADS * IDX_DIM
    tiles_per_seg = seg_len // tm
    kern = functools.partial(_tok_a_kernel, alpha=alpha, tiles_per_seg=tiles_per_seg,
                             f_chunk=256, d_conv=d_conv)
    return pl.pallas_call(
        kern,
        grid=(n_tiles,),
        in_specs=[row(d),
                  pl.BlockSpec((1, 8, d_conv), lambda i: (i // tiles_per_seg, 0, 0)),
                  _const_spec(wg.shape), _const_spec(wu.shape), _const_spec(wd.shape),
                  _const_spec(g.shape), _const_spec(b.shape), _const_spec(win.shape),
                  _const_spec(cw.shape)],
        out_specs=[row(d), row(d_conv), row(nq), row(nkv), row(nkv), row(nqi), row(LANES),
                   pl.BlockSpec((1, 8, d_conv), lambda i: (i, 0, 0))],
        out_shape=[jax.ShapeDtypeStruct((n, d), F32),
                   jax.ShapeDtypeStruct((n, d_conv), F32),
                   jax.ShapeDtypeStruct((n, nq), BF16),
                   jax.ShapeDtypeStruct((n, nkv), F32),
                   jax.ShapeDtypeStruct((n, nkv), F32),
                   jax.ShapeDtypeStruct((n, nqi), BF16),
                   jax.ShapeDtypeStruct((n, LANES), F32),
                   jax.ShapeDtypeStruct((n_tiles, 8, d_conv), F32)],
        scratch_shapes=[pltpu.VMEM((tm, d), F32), pltpu.VMEM((tm + 8, d_conv), F32)],
        compiler_params=pltpu.CompilerParams(dimension_semantics=("arbitrary",),
                                             vmem_limit_bytes=VMEM_LIMIT),
        name="tok_a",
    )(x, cprev, wg, wu, wd, g, b, win, cw)


def _memkv_kernel(m_ref, w_ref, o_ref):
    o_ref[0] = _dot(m_ref[...].astype(BF16), w_ref[0])


def _memkv(mem, w2):
    n, d = mem.shape
    dout = w2.shape[2]
    return pl.pallas_call(
        _memkv_kernel,
        grid=(2,),
        in_specs=[pl.BlockSpec((n, d), lambda j: (0, 0)),
                  pl.BlockSpec((1, d, dout), lambda j: (j, 0, 0))],
        out_specs=pl.BlockSpec((1, n, dout), lambda j: (j, 0, 0)),
        out_shape=jax.ShapeDtypeStruct((2, n, dout), F32),
        compiler_params=pltpu.CompilerParams(dimension_semantics=("arbitrary",),
                                             vmem_limit_bytes=VMEM_LIMIT),
        name="memkv",
    )(mem, w2)


def _key_to_float(key):
    bits = jnp.where(key >= 0, key, key ^ 0x7FFFFFFF)
    return lax.bitcast_convert_type(bits, F32)


def _lane_fold(m):
    part = m[:, 0:LANES]
    for c in range(1, m.shape[1] // LANES):
        part = part + m[:, c * LANES:(c + 1) * LANES]
    return part


def _dsa_kernel(q_ref, qi_ref, wi_ref, kit_ref, kt_ref, v_ref, o_ref, sc_ref,
                *, pos0, s_real, ts, top_k, wi_scale):
    tq = q_ref.shape[0]
    q0 = pl.program_id(1) * tq
    n_adm = jnp.minimum(((pos0 + q0 + tq - 1) // CHUNK + 1) * CHUNK, s_real)
    nkt = (n_adm + ts - 1) // ts
    k_f = float(top_k)

    def tile_off(j):
        return pl.multiple_of(j * ts, ts)

    def key_index(j):
        return j * ts + lax.broadcasted_iota(I32, (1, ts), 1)

    qi = qi_ref[...]
    wi = wi_ref[...] * wi_scale
    q_chunk = lax.shift_right_arithmetic(
        pos0 + q0 + lax.broadcasted_iota(I32, (tq, 1), 0), 6)

    def score_body(j, carry):
        off = tile_off(j)
        kit = kit_ref[:, pl.ds(off, ts)]
        acc = None
        for h in range(IDX_HEADS):
            logit = _dot(qi[:, h * IDX_DIM:(h + 1) * IDX_DIM], kit)
            term = wi[:, h:h + 1] * jnp.maximum(logit, 0.0)
            acc = term if acc is None else acc + term
        kidx = key_index(j)
        admissible = (lax.shift_right_arithmetic(kidx, 6) <= q_chunk) & (kidx < s_real)
        sc_ref[:, pl.ds(off, ts)] = jnp.where(admissible, acc, -jnp.inf)
        return carry

    lax.fori_loop(0, nkt, score_body, 0)

    def count_rows(pred):
        def body(j, a):
            s = sc_ref[:, pl.ds(tile_off(j), ts)]
            return a + _lane_fold(jnp.where(pred(s, j), 1.0, 0.0))
        a = lax.fori_loop(0, nkt, body, jnp.zeros((tq, LANES), F32))
        return jnp.sum(a, axis=1, keepdims=True)

    def any_row(flag):
        return jnp.max(jnp.where(flag, 1.0, 0.0)) > 0.5

    def bis_cond(c):
        lo, hi, _ = c
        return any_row(lo + 1 < hi)

    def bis_body(c):
        lo, hi, cnt_lo = c
        mid = (lax.shift_right_arithmetic(lo, 1) + lax.shift_right_arithmetic(hi, 1)
               + (lo & hi & 1))
        thr = _key_to_float(mid)
        cnt = count_rows(lambda s, j: s >= thr)
        ge = cnt >= k_f
        exact = cnt == k_f
        lo_n = jnp.where(ge, mid, lo)
        hi_n = jnp.where(exact, mid + 1, jnp.where(ge, hi, mid))
        return lo_n, hi_n, jnp.where(ge, cnt, cnt_lo)

    lo0 = jnp.full((tq, 1), KEY_NEG_INF, I32)
    hi0 = jnp.full((tq, 1), KEY_POS_INF + 1, I32)
    cnt0 = jnp.zeros((tq, 1), F32) + (nkt * ts).astype(F32)
    lo, _, cnt_lo = lax.while_loop(bis_cond, bis_body, (lo0, hi0, cnt0))
    thr = _key_to_float(lo)
    excess = jnp.where(thr == -jnp.inf, 0.0, cnt_lo - k_f)

    @pl.when(any_row(excess > 0.5))
    def _():
        def tie_cond(c):
            jl, jh = c
            return any_row(jl + 1 < jh)

        def tie_body(c):
            jl, jh = c
            mid = lax.shift_right_arithmetic(jl + jh, 1)
            cnt = count_rows(lambda s, j: (s == thr) & (key_index(j) >= mid))
            ge = cnt >= excess
            return jnp.where(ge, mid, jl), jnp.where(ge, jh, mid)

        jl, _ = lax.while_loop(tie_cond, tie_body,
                               (jnp.zeros((tq, 1), I32), jnp.zeros((tq, 1), I32) + nkt * ts))

        def drop_body(j, carry):
            off = tile_off(j)
            s = sc_ref[:, pl.ds(off, ts)]
            drop = (s == thr) & (key_index(j) >= jl) & (excess > 0.5)
            sc_ref[:, pl.ds(off, ts)] = jnp.where(drop, -jnp.inf, s)
            return carry

        lax.fori_loop(0, nkt, drop_body, 0)

    thr_fin = jnp.maximum(thr, F32_LOWEST)

    def bias_body(j, carry):
        off = tile_off(j)
        s = sc_ref[:, pl.ds(off, ts)]
        sc_ref[:, pl.ds(off, ts)] = jnp.where(s >= thr_fin, 0.0, NEG_BIG)
        return carry

    lax.fori_loop(0, nkt, bias_body, 0)

    q = q_ref[...]
    for h in range(N_HEADS):
        g = h // GROUP
        qh = q[:, h * HEAD_DIM:(h + 1) * HEAD_DIM]

        def att_body(j, carry, g=g, qh=qh):
            m, l, acc = carry
            off = tile_off(j)
            s = _dot(qh, kt_ref[g, :, pl.ds(off, ts)]) + sc_ref[:, pl.ds(off, ts)]
            m_new = jnp.maximum(m, jnp.max(s, axis=1, keepdims=True))
            a = jnp.exp(m - m_new)
            p = jnp.exp(s - m_new)
            l_new = a * l + jnp.sum(p, axis=1, keepdims=True)
            acc_new = a * acc + _dot(p.astype(BF16), v_ref[g, pl.ds(off, ts), :])
            return m_new, l_new, acc_new

        m0 = jnp.full((tq, 1), NEG_BIG, F32)
        l0 = jnp.zeros((tq, 1), F32)
        acc0 = jnp.zeros((tq, HEAD_DIM), F32)
        _, l, acc = lax.fori_loop(0, nkt, att_body, (m0, l0, acc0))
        o_ref[:, h * HEAD_DIM:(h + 1) * HEAD_DIM] = acc / l


def _dsa(q, qi, wi, kit, kt, v, *, pos0, s_real, tq, ts, top_k):
    nb, t, nq = q.shape
    s_pad = kit.shape[2]
    assert s_pad % ts == 0 and t % tq == 0 and top_k <= ts and top_k <= s_real
    kern = functools.partial(_dsa_kernel, pos0=pos0, s_real=s_real, ts=ts, top_k=top_k,
                             wi_scale=(IDX_HEADS * IDX_DIM) ** -0.5)
    return pl.pallas_call(
        kern,
        grid=(nb, t // tq),
        in_specs=[pl.BlockSpec((None, tq, nq), lambda b, i: (b, i, 0)),
                  pl.BlockSpec((None, tq, qi.shape[2]), lambda b, i: (b, i, 0)),
                  pl.BlockSpec((None, tq, wi.shape[2]), lambda b, i: (b, i, 0)),
                  pl.BlockSpec((None, IDX_DIM, s_pad), lambda b, i: (b, 0, 0)),
                  pl.BlockSpec((None, N_KV_HEADS, HEAD_DIM, s_pad), lambda b, i: (b, 0, 0, 0)),
                  pl.BlockSpec((None, N_KV_HEADS, s_pad, HEAD_DIM), lambda b, i: (b, 0, 0, 0))],
        out_specs=pl.BlockSpec((None, tq, nq), lambda b, i: (b, i, 0)),
        out_shape=jax.ShapeDtypeStruct((nb, t, nq), F32),
        scratch_shapes=[pltpu.VMEM((tq, s_pad), F32)],
        compiler_params=pltpu.CompilerParams(dimension_semantics=("arbitrary", "arbitrary"),
                                             vmem_limit_bytes=VMEM_LIMIT),
        name="dsa",
    )(q, qi, wi, kit, kt, v)


def _tok_b_kernel(x1_ref, yc_ref, ya_ref, mk_ref, mv_ref, wmo_ref, g2_ref, b2_ref,
                  wq_ref, wo_ref, g3_ref, b3_ref, wg_ref, wu_ref, wd_ref, g4_ref, b4_ref,
                  y_ref, acc_ref, *, alpha, f_chunk):
    x1 = x1_ref[...]
    mixed = jnp.concatenate([yc_ref[...], ya_ref[...]], axis=-1).astype(BF16)
    x2 = _layer_norm(alpha * x1 + _dot(mixed, wmo_ref[...]), g2_ref[...], b2_ref[...])

    d = x2.shape[1]
    dh = d // MEM_HEADS
    qm = (_dot(x2.astype(BF16), wq_ref[...]) * (dh ** -0.5)).astype(BF16)
    mk = mk_ref[...].astype(BF16)
    mv = mv_ref[...].astype(BF16)
    heads = []
    for h in range(MEM_HEADS):
        sl = slice(h * dh, (h + 1) * dh)
        s = _dot_nt(qm[:, sl], mk[:, sl])
        e = jnp.exp(s - jnp.max(s, axis=-1, keepdims=True))
        p = e / jnp.sum(e, axis=-1, keepdims=True)
        heads.append(_dot(p.astype(BF16), mv[:, sl]))
    o = jnp.concatenate(heads, axis=-1).astype(BF16)
    x3 = _layer_norm(alpha * x2 + _dot(o, wo_ref[...]), g3_ref[...], b3_ref[...])

    _swiglu_into(acc_ref, x3.astype(BF16), wg_ref, wu_ref, wd_ref, f_chunk)
    y_ref[...] = _layer_norm(alpha * x3 + 0.5 * acc_ref[...], g4_ref[...], b4_ref[...])


def _tok_b(x1, yc, ya, mk, mv, wmo, g2, b2, wq, wo, g3, b3, wg, wu, wd, g4, b4,
           *, alpha, tm, seg_len):
    n, d = x1.shape
    n_mem = mk.shape[1]
    tiles_per_seg = seg_len // tm
    row = lambda w: pl.BlockSpec((tm, w), lambda i: (i, 0))
    mem = pl.BlockSpec((None, n_mem, d), lambda i: (i // tiles_per_seg, 0, 0))
    consts = [wmo, g2, b2, wq, wo, g3, b3, wg, wu, wd, g4, b4]
    kern = functools.partial(_tok_b_kernel, alpha=alpha, f_chunk=256)
    return pl.pallas_call(
        kern,
        grid=(n // tm,),
        in_specs=[row(d), row(yc.shape[1]), row(ya.shape[1]), mem, mem]
                 + [_const_spec(c.shape) for c in consts],
        out_specs=row(d),
        out_shape=jax.ShapeDtypeStruct((n, d), F32),
        scratch_shapes=[pltpu.VMEM((tm, d), F32)],
        compiler_params=pltpu.CompilerParams(dimension_semantics=("arbitrary",),
                                             vmem_limit_bytes=VMEM_LIMIT),
        name="tok_b",
    )(x1, yc, ya, mk, mv, *consts)


def _round_up(x, m):
    return (x + m - 1) // m * m


def _layer(x, conv_prev, k_past, v_past, ik_past, mem_k, mem_v, p, *, alpha, tm, tq, ts):
    b, t, d = x.shape
    past = k_past.shape[1]
    d_conv = p["conv_w"].shape[1]
    top_k = min(TOPK_MAX, (past + t) // 4)

    cprev = jnp.pad(conv_prev, ((0, 0), (6, 0), (0, 0)))
    x1, yconv, q, k, v, qi, tail, ulast = _tok_a(
        x.reshape(b * t, d), cprev, p["ffn1_gate"], p["ffn1_up"], p["ffn1_down"],
        p["ln1_g"], p["ln1_b"], p["w_mix_in"], p["conv_w"], alpha=alpha, tm=tm, seg_len=t)
    new_conv = ulast.reshape(b, t // tm, 8, d_conv)[:, -1, 6:8, :]
    k = k.reshape(b, t, N_KV_HEADS, HEAD_DIM)
    v = v.reshape(b, t, N_KV_HEADS, HEAD_DIM)
    ki = tail[:, :IDX_DIM].reshape(b, t, IDX_DIM)
    wi = tail[:, IDX_DIM:IDX_DIM + IDX_HEADS].reshape(b, t, IDX_HEADS)

    s_real = past + t
    s_pad = _round_up(s_real, ts)
    pad = lambda a: jnp.pad(a, ((0, 0), (0, s_pad - s_real)) + ((0, 0),) * (a.ndim - 2))
    k_all = pad(jnp.concatenate([k_past, k], axis=1)).astype(BF16)
    v_all = pad(jnp.concatenate([v_past, v], axis=1)).astype(BF16)
    ki_all = pad(jnp.concatenate([ik_past, ki], axis=1)).astype(BF16)
    y_attn = _dsa(q.reshape(b, t, -1), qi.reshape(b, t, -1), wi,
                  ki_all.transpose(0, 2, 1), k_all.transpose(0, 2, 3, 1), v_all.transpose(0, 2, 1, 3),
                  pos0=past, s_real=s_real, tq=tq, ts=ts, top_k=top_k)

    y = _tok_b(x1, yconv, y_attn.reshape(b * t, -1), mem_k.reshape(b, mem_k.shape[1], d),
               mem_v.reshape(b, mem_v.shape[1], d),
               p["w_mix_out"], p["ln2_g"], p["ln2_b"], p["w_mem_q"], p["w_mem_o"],
               p["ln3_g"], p["ln3_b"], p["ffn2_gate"], p["ffn2_up"], p["ffn2_down"],
               p["ln4_g"], p["ln4_b"], alpha=alpha, tm=tm, seg_len=t)
    return y.reshape(b, t, d), new_conv, k, v, ki


def kernel(x_prompt, x_sample, cache_conv, cache_k, cache_v, cache_idx_k, cache_mem_k, cache_mem_v,
           mem_prompt, ffn1_gate, ffn1_up, ffn1_down, ln1_g, ln1_b, w_mix_in, conv_w, w_mix_out,
           ln2_g, ln2_b, w_mem_q, w_mem_k, w_mem_v, w_mem_o, ln3_g, ln3_b,
           ffn2_gate, ffn2_up, ffn2_down, ln4_g, ln4_b):
    depth = ffn1_gate.shape[0]
    alpha = (2.0 * depth) ** 0.25
    b_p, t_p, d = x_prompt.shape
    dt = x_prompt.dtype
    d_conv = conv_w.shape[2]
    n_mem = mem_prompt.shape[1]
    bf = lambda w: w.astype(BF16)
    vec = lambda a: a.reshape(1, -1)

    y_p, y_s = x_prompt, x_sample
    outs_p = [[] for _ in range(6)]
    outs_s = [[] for _ in range(4)]
    for l in range(depth):
        d_in = w_mix_in.shape[2]
        p = dict(ffn1_gate=bf(ffn1_gate[l]), ffn1_up=bf(ffn1_up[l]), ffn1_down=bf(ffn1_down[l]),
                 ln1_g=vec(ln1_g[l]), ln1_b=vec(ln1_b[l]),
                 w_mix_in=jnp.pad(bf(w_mix_in[l]),
                                  ((0, 0), (0, _round_up(d_in, LANES) - d_in))),
                 conv_w=conv_w[l], w_mix_out=bf(w_mix_out[l]),
                 ln2_g=vec(ln2_g[l]), ln2_b=vec(ln2_b[l]),
                 w_mem_q=bf(w_mem_q[l]), w_mem_o=bf(w_mem_o[l]),
                 ln3_g=vec(ln3_g[l]), ln3_b=vec(ln3_b[l]),
                 ffn2_gate=bf(ffn2_gate[l]), ffn2_up=bf(ffn2_up[l]), ffn2_down=bf(ffn2_down[l]),
                 ln4_g=vec(ln4_g[l]), ln4_b=vec(ln4_b[l]))
        mem_kv = _memkv(mem_prompt.reshape(b_p * n_mem, d),
                        jnp.stack([bf(w_mem_k[l]), bf(w_mem_v[l])]))
        mem_k = mem_kv[0].reshape(b_p, n_mem, MEM_HEADS, d // MEM_HEADS)
        mem_v = mem_kv[1].reshape(b_p, n_mem, MEM_HEADS, d // MEM_HEADS)
        y_p, c_new, k_new, v_new, ik_new = _layer(
            y_p, jnp.zeros((b_p, 2, d_conv), dt),
            jnp.zeros((b_p, 0, N_KV_HEADS, HEAD_DIM), dt), jnp.zeros((b_p, 0, N_KV_HEADS, HEAD_DIM), dt),
            jnp.zeros((b_p, 0, IDX_DIM), dt), mem_k, mem_v, p,
            alpha=alpha, tm=512, tq=128, ts=256)
        for lst, a in zip(outs_p, (c_new, k_new, v_new, ik_new, mem_k, mem_v)):
            lst.append(a)
        y_s, c_new, k_new, v_new, ik_new = _layer(
            y_s, cache_conv[l], cache_k[l], cache_v[l], cache_idx_k[l],
            cache_mem_k[l], cache_mem_v[l], p,
            alpha=alpha, tm=x_sample.shape[1], tq=x_sample.shape[1], ts=256)
        for lst, a in zip(outs_s, (c_new, k_new, v_new, ik_new)):
            lst.append(a)
    return (y_p, y_s, *[jnp.stack(o) for o in outs_p], *[jnp.stack(o) for o in outs_s])
```

```python
import functools

import jax
import jax.numpy as jnp
from jax import lax
from jax.experimental import pallas as pl
from jax.experimental.pallas import tpu as pltpu

F32 = jnp.float32
BF16 = jnp.bfloat16
I32 = jnp.int32

CHUNK = 64
N_HEADS = 8
HEAD_DIM = 64
N_KV_HEADS = 2
GROUP = N_HEADS // N_KV_HEADS
IDX_HEADS = 8
IDX_DIM = 64
TOPK_MAX = 256
MEM_HEADS = 4
LN_EPS = 1e-5

LANES = 128
VMEM_LIMIT = 56 * 1024 * 1024
NEG_BIG = -1e30
F32_LOWEST = -3.4028234663852886e38
KEY_NEG_INF = -2139095041
KEY_POS_INF = 2139095040


def _dot(a, b):
    return jnp.dot(a, b, preferred_element_type=F32)


def _dot_nt(a, b):
    return lax.dot_general(a, b, (((1,), (1,)), ((), ())), preferred_element_type=F32)


def _layer_norm(z, g, b):
    mu = jnp.mean(z, axis=-1, keepdims=True)
    d = z - mu
    var = jnp.mean(d * d, axis=-1, keepdims=True)
    return d * lax.rsqrt(var + LN_EPS) * g + b


def _swiglu_into(acc_ref, xb, wg_ref, wu_ref, wd_ref, f_chunk):
    d_ff = wg_ref.shape[1]
    for c in range(d_ff // f_chunk):
        sl = slice(c * f_chunk, (c + 1) * f_chunk)
        g = _dot(xb, wg_ref[:, sl])
        u = _dot(xb, wu_ref[:, sl])
        h = (g * (1.0 / (1.0 + jnp.exp(-g)))) * u
        part = _dot(h.astype(BF16), wd_ref[sl, :])
        if c == 0:
            acc_ref[...] = part
        else:
            acc_ref[...] += part


def _tok_a_kernel(x_ref, cprev_ref, wg_ref, wu_ref, wd_ref, g_ref, b_ref, win_ref, cw_ref,
                  x1_ref, yconv_ref, q_ref, k_ref, v_ref, qi_ref, tail_ref, ulast_ref,
                  acc_ref, ubuf_ref, *, alpha, tiles_per_seg, f_chunk, d_conv):
    i = pl.program_id(0)
    tm = x_ref.shape[0]
    x = x_ref[...]
    _swiglu_into(acc_ref, x.astype(BF16), wg_ref, wu_ref, wd_ref, f_chunk)
    x1 = _layer_norm(alpha * x + 0.5 * acc_ref[...], g_ref[...], b_ref[...])
    x1_ref[...] = x1
    x1b = x1.astype(BF16)

    c0 = 0
    h = _dot(x1b, win_ref[:, c0:c0 + d_conv]); c0 += d_conv
    gate_b = _dot(x1b, win_ref[:, c0:c0 + d_conv]); c0 += d_conv
    gate_c = _dot(x1b, win_ref[:, c0:c0 + d_conv]); c0 += d_conv
    nq = N_HEADS * HEAD_DIM
    q_ref[...] = (_dot(x1b, win_ref[:, c0:c0 + nq]) * (HEAD_DIM ** -0.5)).astype(BF16); c0 += nq
    nkv = N_KV_HEADS * HEAD_DIM
    k_ref[...] = _dot(x1b, win_ref[:, c0:c0 + nkv]); c0 += nkv
    v_ref[...] = _dot(x1b, win_ref[:, c0:c0 + nkv]); c0 += nkv
    nqi = IDX_HEADS * IDX_DIM
    qi_ref[...] = _dot(x1b, win_ref[:, c0:c0 + nqi]).astype(BF16); c0 += nqi
    tail_ref[...] = _dot(x1b, win_ref[:, c0:c0 + LANES])

    u = gate_c * h

    @pl.when(i % tiles_per_seg == 0)
    def _():
        ubuf_ref[0:8, :] = cprev_ref[0]

    ubuf_ref[8:tm + 8, :] = u
    um1 = ubuf_ref[7:tm + 7, :]
    um2 = ubuf_ref[6:tm + 6, :]
    cw = cw_ref[...]
    conv = cw[0:1, :] * um2 + cw[1:2, :] * um1 + cw[2:3, :] * u
    yconv_ref[...] = gate_b * conv
    last8 = ubuf_ref[tm:tm + 8, :]
    ulast_ref[0] = last8
    ubuf_ref[0:8, :] = last8


def _const_spec(shape):
    nd = len(shape)
    return pl.BlockSpec(shape, lambda *_: (0,) * nd, pipeline_mode=pl.Buffered(1))


def _tok_a(x, cprev, wg, wu, wd, g, b, win, cw, *, alpha, tm, seg_len):
    n, d = x.shape
    d_ff = wg.shape[1]
    d_conv = cw.shape[1]
    n_tiles = n // tm
    row = lambda w: pl.BlockSpec((tm, w), lambda i: (i, 0))
    nq, nkv, nqi = N_HEADS * HEAD_DIM, N_KV_HEADS * HEAD_DIM, IDX_HEADS * IDX_DIM
    tiles_per_seg = seg_len // tm
    kern = functools.partial(_tok_a_kernel, alpha=alpha, tiles_per_seg=tiles_per_seg,
                             f_chunk=256, d_conv=d_conv)
    return pl.pallas_call(
        kern,
        grid=(n_tiles,),
        in_specs=[row(d),
                  pl.BlockSpec((1, 8, d_conv), lambda i: (i // tiles_per_seg, 0, 0)),
                  _const_spec(wg.shape), _const_spec(wu.shape), _const_spec(wd.shape),
                  _const_spec(g.shape), _const_spec(b.shape), _const_spec(win.shape),
                  _const_spec(cw.shape)],
        out_specs=[row(d), row(d_conv), row(nq), row(nkv), row(nkv), row(nqi), row(LANES),
                   pl.BlockSpec((1, 8, d_conv), lambda i: (i, 0, 0))],
        out_shape=[jax.ShapeDtypeStruct((n, d), F32),
                   jax.ShapeDtypeStruct((n, d_conv), F32),
                   jax.ShapeDtypeStruct((n, nq), BF16),
                   jax.ShapeDtypeStruct((n, nkv), F32),
                   jax.ShapeDtypeStruct((n, nkv), F32),
                   jax.ShapeDtypeStruct((n, nqi), BF16),
                   jax.ShapeDtypeStruct((n, LANES), F32),
                   jax.ShapeDtypeStruct((n_tiles, 8, d_conv), F32)],
        scratch_shapes=[pltpu.VMEM((tm, d), F32), pltpu.VMEM((tm + 8, d_conv), F32)],
        compiler_params=pltpu.CompilerParams(dimension_semantics=("arbitrary",),
                                             vmem_limit_bytes=VMEM_LIMIT),
        name="tok_a",
    )(x, cprev, wg, wu, wd, g, b, win, cw)


def _memkv_kernel(m_ref, w_ref, o_ref):
    o_ref[0] = _dot(m_ref[...].astype(BF16), w_ref[0])


def _memkv(mem, w2):
    n, d = mem.shape
    dout = w2.shape[2]
    return pl.pallas_call(
        _memkv_kernel,
        grid=(2,),
        in_specs=[pl.BlockSpec((n, d), lambda j: (0, 0)),
                  pl.BlockSpec((1, d, dout), lambda j: (j, 0, 0))],
        out_specs=pl.BlockSpec((1, n, dout), lambda j: (j, 0, 0)),
        out_shape=jax.ShapeDtypeStruct((2, n, dout), F32),
        compiler_params=pltpu.CompilerParams(dimension_semantics=("arbitrary",),
                                             vmem_limit_bytes=VMEM_LIMIT),
        name="memkv",
    )(mem, w2)


def _key_to_float(key):
    bits = jnp.where(key >= 0, key, key ^ 0x7FFFFFFF)
    return lax.bitcast_convert_type(bits, F32)


def _lane_fold(m):
    part = m[:, 0:LANES]
    for c in range(1, m.shape[1] // LANES):
        part = part + m[:, c * LANES:(c + 1) * LANES]
    return part


def _dsa_kernel(q_ref, qi_ref, wi_ref, kit_ref, kt_ref, v_ref, o_ref,
                sc_ref, qis_ref, wb_ref, qs_ref, m_ref, acc_ref,
                *, pos0, s_real, ts, top_k, wi_scale):
    tq = q_ref.shape[0]
    q0 = pl.program_id(1) * tq
    n_adm = jnp.minimum(((pos0 + q0 + tq - 1) // CHUNK + 1) * CHUNK, s_real)
    nkt = (n_adm + ts - 1) // ts
    k_f = float(top_k)

    def tile_off(j):
        return pl.multiple_of(j * ts, ts)

    def key_index(j):
        return j * ts + lax.broadcasted_iota(I32, (1, ts), 1)

    wi = wi_ref[...] * wi_scale
    for h in range(IDX_HEADS):
        qis_ref[h] = qi_ref[:, h * IDX_DIM:(h + 1) * IDX_DIM]
        wb_ref[h] = jnp.broadcast_to(wi[:, h:h + 1], (tq, LANES))
    q_chunk = lax.shift_right_arithmetic(
        pos0 + q0 + lax.broadcasted_iota(I32, (tq, 1), 0), 6)

    def lane_tile(a):
        return jnp.concatenate([a] * (ts // LANES), axis=1)

    def score_body(j, carry):
        off = tile_off(j)
        kit = kit_ref[:, pl.ds(off, ts)]
        acc = None
        for h in range(IDX_HEADS):
            term = lane_tile(wb_ref[h]) * jnp.maximum(_dot(qis_ref[h], kit), 0.0)
            acc = term if acc is None else acc + term
        kidx = key_index(j)
        admissible = (lax.shift_right_arithmetic(kidx, 6) <= q_chunk) & (kidx < s_real)
        sc_ref[:, pl.ds(off, ts)] = jnp.where(admissible, acc, -jnp.inf)
        return carry

    lax.fori_loop(0, nkt, score_body, 0)

    def count_rows(pred):
        def body(j, a):
            s = sc_ref[:, pl.ds(tile_off(j), ts)]
            return a + _lane_fold(jnp.where(pred(s, j), 1.0, 0.0))
        a = lax.fori_loop(0, nkt, body, jnp.zeros((tq, LANES), F32))
        return jnp.sum(a, axis=1, keepdims=True)

    def any_row(flag):
        return jnp.max(jnp.where(flag, 1.0, 0.0)) > 0.5

    def bis_cond(c):
        lo, hi, _ = c
        return any_row(lo + 1 < hi)

    def bis_body(c):
        lo, hi, cnt_lo = c
        mid = (lax.shift_right_arithmetic(lo, 1) + lax.shift_right_arithmetic(hi, 1)
               + (lo & hi & 1))
        thr = _key_to_float(mid)
        cnt = count_rows(lambda s, j: s >= thr)
        ge = cnt >= k_f
        exact = cnt == k_f
        lo_n = jnp.where(ge, mid, lo)
        hi_n = jnp.where(exact, mid + 1, jnp.where(ge, hi, mid))
        return lo_n, hi_n, jnp.where(ge, cnt, cnt_lo)

    lo0 = jnp.full((tq, 1), KEY_NEG_INF, I32)
    hi0 = jnp.full((tq, 1), KEY_POS_INF + 1, I32)
    cnt0 = jnp.zeros((tq, 1), F32) + (nkt * ts).astype(F32)
    lo, _, cnt_lo = lax.while_loop(bis_cond, bis_body, (lo0, hi0, cnt0))
    thr = _key_to_float(lo)
    excess = jnp.where(thr == -jnp.inf, 0.0, cnt_lo - k_f)

    @pl.when(any_row(excess > 0.5))
    def _():
        def tie_cond(c):
            jl, jh = c
            return any_row(jl + 1 < jh)

        def tie_body(c):
            jl, jh = c
            mid = lax.shift_right_arithmetic(jl + jh, 1)
            cnt = count_rows(lambda s, j: (s == thr) & (key_index(j) >= mid))
            ge = cnt >= excess
            return jnp.where(ge, mid, jl), jnp.where(ge, jh, mid)

        jl, _ = lax.while_loop(tie_cond, tie_body,
                               (jnp.zeros((tq, 1), I32), jnp.zeros((tq, 1), I32) + nkt * ts))

        def drop_body(j, carry):
            off = tile_off(j)
            s = sc_ref[:, pl.ds(off, ts)]
            drop = (s == thr) & (key_index(j) >= jl) & (excess > 0.5)
            sc_ref[:, pl.ds(off, ts)] = jnp.where(drop, -jnp.inf, s)
            return carry

        lax.fori_loop(0, nkt, drop_body, 0)

    thr_fin = jnp.maximum(thr, F32_LOWEST)

    def bias_body(j, carry):
        off = tile_off(j)
        s = sc_ref[:, pl.ds(off, ts)]
        sc_ref[:, pl.ds(off, ts)] = jnp.where(s >= thr_fin, 0.0, NEG_BIG)
        return carry

    lax.fori_loop(0, nkt, bias_body, 0)

    for h in range(N_HEADS):
        qs_ref[h] = q_ref[:, h * HEAD_DIM:(h + 1) * HEAD_DIM]
    m_ref[...] = jnp.full(m_ref.shape, NEG_BIG, F32)
    acc_ref[...] = jnp.zeros(acc_ref.shape, F32)

    def att_body(j, carry):
        off = tile_off(j)
        bias = sc_ref[:, pl.ds(off, ts)]
        for h in range(N_HEADS):
            g = h // GROUP
            s = _dot(qs_ref[h], kt_ref[g, :, pl.ds(off, ts)]) + bias
            m_old = m_ref[h]
            m_new = jnp.maximum(m_old, jnp.max(s, axis=1, keepdims=True))
            p = jnp.exp(s - lane_tile(m_new)).astype(BF16)
            acc_ref[h] = (jnp.exp(m_old - m_new) * acc_ref[h]
                          + _dot(p, v_ref[g, pl.ds(off, ts), :]))
            m_ref[h] = m_new
        return carry

    lax.fori_loop(0, nkt, att_body, 0)
    for h in range(N_HEADS):
        acc = acc_ref[h]
        o_ref[:, h * HEAD_DIM:(h + 1) * HEAD_DIM] = (
            acc[:, 0:HEAD_DIM] / acc[:, HEAD_DIM:HEAD_DIM + 1])


def _dsa(q, qi, wi, kit, kt, v, *, pos0, s_real, tq, ts, top_k):
    nb, t, nq = q.shape
    s_pad = kit.shape[2]
    tile3 = lambda w, dt: pltpu.VMEM((N_HEADS, tq, w), dt)
    assert s_pad % ts == 0 and t % tq == 0 and top_k <= ts and top_k <= s_real
    kern = functools.partial(_dsa_kernel, pos0=pos0, s_real=s_real, ts=ts, top_k=top_k,
                             wi_scale=(IDX_HEADS * IDX_DIM) ** -0.5)
    return pl.pallas_call(
        kern,
        grid=(nb, t // tq),
        in_specs=[pl.BlockSpec((None, tq, nq), lambda b, i: (b, i, 0)),
                  pl.BlockSpec((None, tq, qi.shape[2]), lambda b, i: (b, i, 0)),
                  pl.BlockSpec((None, tq, wi.shape[2]), lambda b, i: (b, i, 0)),
                  pl.BlockSpec((None, IDX_DIM, s_pad), lambda b, i: (b, 0, 0)),
                  pl.BlockSpec((None, N_KV_HEADS, HEAD_DIM, s_pad), lambda b, i: (b, 0, 0, 0)),
                  pl.BlockSpec((None, N_KV_HEADS, s_pad, LANES), lambda b, i: (b, 0, 0, 0))],
        out_specs=pl.BlockSpec((None, tq, nq), lambda b, i: (b, i, 0)),
        out_shape=jax.ShapeDtypeStruct((nb, t, nq), F32),
        scratch_shapes=[pltpu.VMEM((tq, s_pad), F32), tile3(IDX_DIM, BF16), tile3(LANES, F32),
                        tile3(HEAD_DIM, BF16), tile3(LANES, F32), tile3(LANES, F32)],
        compiler_params=pltpu.CompilerParams(dimension_semantics=("arbitrary", "arbitrary"),
                                             vmem_limit_bytes=VMEM_LIMIT),
        name="dsa",
    )(q, qi, wi, kit, kt, v)


def _tok_b_kernel(x1_ref, yc_ref, ya_ref, mk_ref, mv_ref, wmo_ref, g2_ref, b2_ref,
                  wq_ref, wo_ref, g3_ref, b3_ref, wg_ref, wu_ref, wd_ref, g4_ref, b4_ref,
                  y_ref, acc_ref, *, alpha, f_chunk):
    x1 = x1_ref[...]
    mixed = jnp.concatenate([yc_ref[...], ya_ref[...]], axis=-1).astype(BF16)
    x2 = _layer_norm(alpha * x1 + _dot(mixed, wmo_ref[...]), g2_ref[...], b2_ref[...])

    d = x2.shape[1]
    dh = d // MEM_HEADS
    qm = (_dot(x2.astype(BF16), wq_ref[...]) * (dh ** -0.5)).astype(BF16)
    mk = mk_ref[...].astype(BF16)
    mv = mv_ref[...].astype(BF16)
    heads = []
    for h in range(MEM_HEADS):
        sl = slice(h * dh, (h + 1) * dh)
        s = _dot_nt(qm[:, sl], mk[:, sl])
        e = jnp.exp(s - jnp.max(s, axis=-1, keepdims=True))
        p = e / jnp.sum(e, axis=-1, keepdims=True)
        heads.append(_dot(p.astype(BF16), mv[:, sl]))
    o = jnp.concatenate(heads, axis=-1).astype(BF16)
    x3 = _layer_norm(alpha * x2 + _dot(o, wo_ref[...]), g3_ref[...], b3_ref[...])

    _swiglu_into(acc_ref, x3.astype(BF16), wg_ref, wu_ref, wd_ref, f_chunk)
    y_ref[...] = _layer_norm(alpha * x3 + 0.5 * acc_ref[...], g4_ref[...], b4_ref[...])


def _tok_b(x1, yc, ya, mk, mv, wmo, g2, b2, wq, wo, g3, b3, wg, wu, wd, g4, b4,
           *, alpha, tm, seg_len):
    n, d = x1.shape
    n_mem = mk.shape[1]
    tiles_per_seg = seg_len // tm
    row = lambda w: pl.BlockSpec((tm, w), lambda i: (i, 0))
    mem = pl.BlockSpec((None, n_mem, d), lambda i: (i // tiles_per_seg, 0, 0))
    consts = [wmo, g2, b2, wq, wo, g3, b3, wg, wu, wd, g4, b4]
    kern = functools.partial(_tok_b_kernel, alpha=alpha, f_chunk=256)
    return pl.pallas_call(
        kern,
        grid=(n // tm,),
        in_specs=[row(d), row(yc.shape[1]), row(ya.shape[1]), mem, mem]
                 + [_const_spec(c.shape) for c in consts],
        out_specs=row(d),
        out_shape=jax.ShapeDtypeStruct((n, d), F32),
        scratch_shapes=[pltpu.VMEM((tm, d), F32)],
        compiler_params=pltpu.CompilerParams(dimension_semantics=("arbitrary",),
                                             vmem_limit_bytes=VMEM_LIMIT),
        name="tok_b",
    )(x1, yc, ya, mk, mv, *consts)


def _round_up(x, m):
    return (x + m - 1) // m * m


def _layer(x, conv_prev, k_past, v_past, ik_past, mem_k, mem_v, p, *, alpha, tm, tq, ts):
    b, t, d = x.shape
    past = k_past.shape[1]
    d_conv = p["conv_w"].shape[1]
    top_k = min(TOPK_MAX, (past + t) // 4)

    cprev = jnp.pad(conv_prev, ((0, 0), (6, 0), (0, 0)))
    x1, yconv, q, k, v, qi, tail, ulast = _tok_a(
        x.reshape(b * t, d), cprev, p["ffn1_gate"], p["ffn1_up"], p["ffn1_down"],
        p["ln1_g"], p["ln1_b"], p["w_mix_in"], p["conv_w"], alpha=alpha, tm=tm, seg_len=t)
    new_conv = ulast.reshape(b, t // tm, 8, d_conv)[:, -1, 6:8, :]
    k = k.reshape(b, t, N_KV_HEADS, HEAD_DIM)
    v = v.reshape(b, t, N_KV_HEADS, HEAD_DIM)
    ki = tail[:, :IDX_DIM].reshape(b, t, IDX_DIM)
    wi = tail[:, IDX_DIM:IDX_DIM + IDX_HEADS].reshape(b, t, IDX_HEADS)

    s_real = past + t
    s_pad = _round_up(s_real, ts)
    pad = lambda a: jnp.pad(a, ((0, 0), (0, s_pad - s_real)) + ((0, 0),) * (a.ndim - 2))
    k_all = pad(jnp.concatenate([k_past, k], axis=1)).astype(BF16)
    v_all = pad(jnp.concatenate([v_past, v], axis=1)).astype(BF16)
    ki_all = pad(jnp.concatenate([ik_past, ki], axis=1)).astype(BF16)
    ones_col = (jnp.arange(LANES - HEAD_DIM) == 0).astype(BF16)
    v_aug = jnp.concatenate(
        [v_all.transpose(0, 2, 1, 3),
         jnp.broadcast_to(ones_col, (b, N_KV_HEADS, s_pad, LANES - HEAD_DIM))], axis=-1)
    y_attn = _dsa(q.reshape(b, t, -1), qi.reshape(b, t, -1), wi,
                  ki_all.transpose(0, 2, 1), k_all.transpose(0, 2, 3, 1), v_aug,
                  pos0=past, s_real=s_real, tq=tq, ts=ts, top_k=top_k)

    y = _tok_b(x1, yconv, y_attn.reshape(b * t, -1), mem_k.reshape(b, mem_k.shape[1], d),
               mem_v.reshape(b, mem_v.shape[1], d),
               p["w_mix_out"], p["ln2_g"], p["ln2_b"], p["w_mem_q"], p["w_mem_o"],
               p["ln3_g"], p["ln3_b"], p["ffn2_gate"], p["ffn2_up"], p["ffn2_down"],
               p["ln4_g"], p["ln4_b"], alpha=alpha, tm=tm, seg_len=t)
    return y.reshape(b, t, d), new_conv, k, v, ki


def kernel(x_prompt, x_sample, cache_conv, cache_k, cache_v, cache_idx_k, cache_mem_k, cache_mem_v,
           mem_prompt, ffn1_gate, ffn1_up, ffn1_down, ln1_g, ln1_b, w_mix_in, conv_w, w_mix_out,
           ln2_g, ln2_b, w_mem_q, w_mem_k, w_mem_v, w_mem_o, ln3_g, ln3_b,
           ffn2_gate, ffn2_up, ffn2_down, ln4_g, ln4_b):
    depth = ffn1_gate.shape[0]
    alpha = (2.0 * depth) ** 0.25
    b_p, t_p, d = x_prompt.shape
    dt = x_prompt.dtype
    d_conv = conv_w.shape[2]
    n_mem = mem_prompt.shape[1]
    bf = lambda w: w.astype(BF16)
    vec = lambda a: a.reshape(1, -1)

    y_p, y_s = x_prompt, x_sample
    outs_p = [[] for _ in range(6)]
    outs_s = [[] for _ in range(4)]
    for l in range(depth):
        d_in = w_mix_in.shape[2]
        p = dict(ffn1_gate=bf(ffn1_gate[l]), ffn1_up=bf(ffn1_up[l]), ffn1_down=bf(ffn1_down[l]),
                 ln1_g=vec(ln1_g[l]), ln1_b=vec(ln1_b[l]),
                 w_mix_in=jnp.pad(bf(w_mix_in[l]),
                                  ((0, 0), (0, _round_up(d_in, LANES) - d_in))),
                 conv_w=conv_w[l], w_mix_out=bf(w_mix_out[l]),
                 ln2_g=vec(ln2_g[l]), ln2_b=vec(ln2_b[l]),
                 w_mem_q=bf(w_mem_q[l]), w_mem_o=bf(w_mem_o[l]),
                 ln3_g=vec(ln3_g[l]), ln3_b=vec(ln3_b[l]),
                 ffn2_gate=bf(ffn2_gate[l]), ffn2_up=bf(ffn2_up[l]), ffn2_down=bf(ffn2_down[l]),
                 ln4_g=vec(ln4_g[l]), ln4_b=vec(ln4_b[l]))
        mem_kv = _memkv(mem_prompt.reshape(b_p * n_mem, d),
                        jnp.stack([bf(w_mem_k[l]), bf(w_mem_v[l])]))
        mem_k = mem_kv[0].reshape(b_p, n_mem, MEM_HEADS, d // MEM_HEADS)
        mem_v = mem_kv[1].reshape(b_p, n_mem, MEM_HEADS, d // MEM_HEADS)
        y_p, c_new, k_new, v_new, ik_new = _layer(
            y_p, jnp.zeros((b_p, 2, d_conv), dt),
            jnp.zeros((b_p, 0, N_KV_HEADS, HEAD_DIM), dt), jnp.zeros((b_p, 0, N_KV_HEADS, HEAD_DIM), dt),
            jnp.zeros((b_p, 0, IDX_DIM), dt), mem_k, mem_v, p,
            alpha=alpha, tm=512, tq=128, ts=256)
        for lst, a in zip(outs_p, (c_new, k_new, v_new, ik_new, mem_k, mem_v)):
            lst.append(a)
        y_s, c_new, k_new, v_new, ik_new = _layer(
            y_s, cache_conv[l], cache_k[l], cache_v[l], cache_idx_k[l],
            cache_mem_k[l], cache_mem_v[l], p,
            alpha=alpha, tm=x_sample.shape[1], tq=x_sample.shape[1], ts=256)
        for lst, a in zip(outs_s, (c_new, k_new, v_new, ik_new)):
            lst.append(a)
    return (y_p, y_s, *[jnp.stack(o) for o in outs_p], *[jnp.stack(o) for o in outs_s])
```

```python
import functools

import jax
import jax.numpy as jnp
from jax import lax
from jax.experimental import pallas as pl
from jax.experimental.pallas import tpu as pltpu

F32 = jnp.float32
BF16 = jnp.bfloat16
I32 = jnp.int32
I16 = jnp.int16

CHUNK = 64
N_HEADS = 8
HEAD_DIM = 64
N_KV_HEADS = 2
GROUP = N_HEADS // N_KV_HEADS
IDX_HEADS = 8
IDX_DIM = 64
TOPK_MAX = 256
MEM_HEADS = 4
LN_EPS = 1e-5

LANES = 128
VMEM_LIMIT = 56 * 1024 * 1024
NEG_BIG = -1e30
F32_LOWEST = -3.4028234663852886e38
I32_MIN = -2147483648


def _dot(a, b):
    return jnp.dot(a, b, preferred_element_type=F32)


def _dot_nt(a, b):
    return lax.dot_general(a, b, (((1,), (1,)), ((), ())), preferred_element_type=F32)


def _layer_norm(z, g, b):
    mu = jnp.mean(z, axis=-1, keepdims=True)
    d = z - mu
    var = jnp.mean(d * d, axis=-1, keepdims=True)
    return d * lax.rsqrt(var + LN_EPS) * g + b


def _swiglu_into(acc_ref, xb, wg_ref, wu_ref, wd_ref, f_chunk):
    d_ff = wg_ref.shape[1]
    for c in range(d_ff // f_chunk):
        sl = slice(c * f_chunk, (c + 1) * f_chunk)
        g = _dot(xb, wg_ref[:, sl])
        u = _dot(xb, wu_ref[:, sl])
        h = (g * (1.0 / (1.0 + jnp.exp(-g)))) * u
        part = _dot(h.astype(BF16), wd_ref[sl, :])
        if c == 0:
            acc_ref[...] = part
        else:
            acc_ref[...] += part


def _tok_a_kernel(x_ref, cprev_ref, wg_ref, wu_ref, wd_ref, g_ref, b_ref, win_ref, cw_ref,
                  x1_ref, yconv_ref, q_ref, k_ref, v_ref, qi_ref, tail_ref, ulast_ref,
                  acc_ref, ubuf_ref, *, alpha, tiles_per_seg, f_chunk, d_conv):
    i = pl.program_id(0)
    tm = x_ref.shape[0]
    x = x_ref[...]
    _swiglu_into(acc_ref, x.astype(BF16), wg_ref, wu_ref, wd_ref, f_chunk)
    x1 = _layer_norm(alpha * x + 0.5 * acc_ref[...], g_ref[...], b_ref[...])
    x1_ref[...] = x1
    x1b = x1.astype(BF16)

    c0 = 0
    h = _dot(x1b, win_ref[:, c0:c0 + d_conv]); c0 += d_conv
    gate_b = _dot(x1b, win_ref[:, c0:c0 + d_conv]); c0 += d_conv
    gate_c = _dot(x1b, win_ref[:, c0:c0 + d_conv]); c0 += d_conv
    nq = N_HEADS * HEAD_DIM
    q_ref[...] = (_dot(x1b, win_ref[:, c0:c0 + nq]) * (HEAD_DIM ** -0.5)).astype(BF16); c0 += nq
    nkv = N_KV_HEADS * HEAD_DIM
    k_ref[...] = _dot(x1b, win_ref[:, c0:c0 + nkv]); c0 += nkv
    v_ref[...] = _dot(x1b, win_ref[:, c0:c0 + nkv]); c0 += nkv
    nqi = IDX_HEADS * IDX_DIM
    qi_ref[...] = _dot(x1b, win_ref[:, c0:c0 + nqi]).astype(BF16); c0 += nqi
    tail_ref[...] = _dot(x1b, win_ref[:, c0:c0 + LANES])

    u = gate_c * h

    @pl.when(i % tiles_per_seg == 0)
    def _():
        ubuf_ref[0:8, :] = cprev_ref[0]

    ubuf_ref[8:tm + 8, :] = u
    um1 = ubuf_ref[7:tm + 7, :]
    um2 = ubuf_ref[6:tm + 6, :]
    cw = cw_ref[...]
    conv = cw[0:1, :] * um2 + cw[1:2, :] * um1 + cw[2:3, :] * u
    yconv_ref[...] = gate_b * conv
    last8 = ubuf_ref[tm:tm + 8, :]
    ulast_ref[0] = last8
    ubuf_ref[0:8, :] = last8


def _const_spec(shape):
    nd = len(shape)
    return pl.BlockSpec(shape, lambda *_: (0,) * nd, pipeline_mode=pl.Buffered(1))


def _tok_a(x, cprev, wg, wu, wd, g, b, win, cw, *, alpha, tm, seg_len):
    n, d = x.shape
    d_ff = wg.shape[1]
    d_conv = cw.shape[1]
    n_tiles = n // tm
    row = lambda w: pl.BlockSpec((tm, w), lambda i: (i, 0))
    nq, nkv, nqi = N_HEADS * HEAD_DIM, N_KV_HEADS * HEAD_DIM, IDX_HEADS * IDX_DIM
    tiles_per_seg = seg_len // tm
    kern = functools.partial(_tok_a_kernel, alpha=alpha, tiles_per_seg=tiles_per_seg,
                             f_chunk=256, d_conv=d_conv)
    return pl.pallas_call(
        kern,
        grid=(n_tiles,),
        in_specs=[row(d),
                  pl.BlockSpec((1, 8, d_conv), lambda i: (i // tiles_per_seg, 0, 0)),
                  _const_spec(wg.shape), _const_spec(wu.shape), _const_spec(wd.shape),
                  _const_spec(g.shape), _const_spec(b.shape), _const_spec(win.shape),
                  _const_spec(cw.shape)],
        out_specs=[row(d), row(d_conv), row(nq), row(nkv), row(nkv), row(nqi), row(LANES),
                   pl.BlockSpec((1, 8, d_conv), lambda i: (i, 0, 0))],
        out_shape=[jax.ShapeDtypeStruct((n, d), F32),
                   jax.ShapeDtypeStruct((n, d_conv), F32),
                   jax.ShapeDtypeStruct((n, nq), BF16),
                   jax.ShapeDtypeStruct((n, nkv), F32),
                   jax.ShapeDtypeStruct((n, nkv), F32),
                   jax.ShapeDtypeStruct((n, nqi), BF16),
                   jax.ShapeDtypeStruct((n, LANES), F32),
                   jax.ShapeDtypeStruct((n_tiles, 8, d_conv), F32)],
        scratch_shapes=[pltpu.VMEM((tm, d), F32), pltpu.VMEM((tm + 8, d_conv), F32)],
        compiler_params=pltpu.CompilerParams(dimension_semantics=("arbitrary",),
                                             vmem_limit_bytes=VMEM_LIMIT),
        name="tok_a",
    )(x, cprev, wg, wu, wd, g, b, win, cw)


def _memkv_kernel(m_ref, w_ref, o_ref):
    o_ref[0] = _dot(m_ref[...].astype(BF16), w_ref[0])


def _memkv(mem, w2):
    n, d = mem.shape
    dout = w2.shape[2]
    return pl.pallas_call(
        _memkv_kernel,
        grid=(2,),
        in_specs=[pl.BlockSpec((n, d), lambda j: (0, 0)),
                  pl.BlockSpec((1, d, dout), lambda j: (j, 0, 0))],
        out_specs=pl.BlockSpec((1, n, dout), lambda j: (j, 0, 0)),
        out_shape=jax.ShapeDtypeStruct((2, n, dout), F32),
        compiler_params=pltpu.CompilerParams(dimension_semantics=("arbitrary",),
                                             vmem_limit_bytes=VMEM_LIMIT),
        name="memkv",
    )(mem, w2)


def _float_to_key(x):
    b = lax.bitcast_convert_type(x, I32)
    return jnp.where(b >= 0, b, I32_MIN - b)


def _key_to_float(key):
    bits = jnp.where(key >= 0, key, I32_MIN - key)
    return lax.bitcast_convert_type(bits, F32)


def _lane_fold(m):
    part = m[:, 0:LANES]
    for c in range(1, m.shape[1] // LANES):
        part = part + m[:, c * LANES:(c + 1) * LANES]
    return part


def _dsa_kernel(q_ref, qi_ref, wi_ref, kit_ref, kt_ref, v_ref, o_ref,
                sc_ref, hi_ref, lo_ref, qis_ref, wb_ref, qs_ref, m_ref, acc_ref,
                *, pos0, s_real, ts, rows, top_k, wi_scale):
    tq = q_ref.shape[0]
    row_blocks = [slice(r * rows, (r + 1) * rows) for r in range(tq // rows)]
    q0 = pl.program_id(1) * tq
    n_adm = jnp.minimum(((pos0 + q0 + tq - 1) // CHUNK + 1) * CHUNK, s_real)
    nkt = (n_adm + ts - 1) // ts
    k_f = float(top_k)

    def tile_off(j):
        return pl.multiple_of(j * ts, ts)

    def key_index(j):
        return j * ts + lax.broadcasted_iota(I32, (1, ts), 1)

    def lane_tile(a):
        return jnp.concatenate([a] * (ts // LANES), axis=1)

    def any_row(flag):
        return jnp.max(jnp.where(flag, 1.0, 0.0)) > 0.5

    wi = wi_ref[...] * wi_scale
    for h in range(IDX_HEADS):
        qis_ref[h] = qi_ref[:, h * IDX_DIM:(h + 1) * IDX_DIM]
        wb_ref[h] = jnp.broadcast_to(wi[:, h:h + 1], (tq, LANES))
    q_chunk = lax.shift_right_arithmetic(
        pos0 + q0 + lax.broadcasted_iota(I32, (tq, 1), 0), 6)

    def score_body(j, carry):
        off = tile_off(j)
        kit = kit_ref[:, pl.ds(off, ts)]
        kidx = key_index(j)
        k_chunk = lax.shift_right_arithmetic(kidx, 6)
        for rs in row_blocks:
            acc = None
            for h in range(IDX_HEADS):
                term = lane_tile(wb_ref[h, rs, :]) * jnp.maximum(_dot(qis_ref[h, rs, :], kit), 0.0)
                acc = term if acc is None else acc + term
            admissible = (k_chunk <= q_chunk[rs]) & (kidx < s_real)
            s = jnp.where(admissible, acc, -jnp.inf)
            sc_ref[rs, pl.ds(off, ts)] = s
            key = _float_to_key(s)
            hi_ref[rs, pl.ds(off, ts)] = lax.shift_right_arithmetic(key, 16).astype(I16)
            lo_ref[rs, pl.ds(off, ts)] = ((key & 0xFFFF) - 32768).astype(I16)
        return carry

    lax.fori_loop(0, nkt, score_body, 0)

    ones_red = jnp.ones((LANES, LANES), BF16)

    def count_ge(ref, c32):
        c = lane_tile(c32.astype(I16))

        def body(j, a):
            x = ref[:, pl.ds(tile_off(j), ts)]
            return a + _lane_fold(jnp.where(x >= c, jnp.ones((), BF16), jnp.zeros((), BF16)))

        a = lax.fori_loop(0, nkt, body, jnp.zeros((tq, LANES), BF16))
        return _dot(a, ones_red)

    def bit(i):
        return lax.shift_left(jnp.int32(1), 15 - i)

    def hi_body(i, c):
        t, c_ok, c_fail = c
        t_try = t + bit(i)
        cnt = count_ge(hi_ref, t_try)
        ok = cnt >= k_f
        return jnp.where(ok, t_try, t), jnp.where(ok, cnt, c_ok), jnp.where(ok, c_fail, cnt)

    total = jnp.zeros((tq, LANES), F32) + (nkt * ts).astype(F32)
    t_hi, c_ok, c_gt = lax.fori_loop(
        0, 16, hi_body, (jnp.full((tq, LANES), -32768, I32), total, jnp.zeros((tq, LANES), F32)))

    need = k_f - c_gt
    t16 = lane_tile(t_hi.astype(I16))

    def cand_body(j, carry):
        off = tile_off(j)
        lo_ref[:, pl.ds(off, ts)] = jnp.where(hi_ref[:, pl.ds(off, ts)] == t16,
                                              lo_ref[:, pl.ds(off, ts)], jnp.int16(-32768))
        return carry

    lax.fori_loop(0, nkt, cand_body, 0)

    def lo_cond(c):
        i, _, _, done = c
        return (i < 16) & any_row(done < 0.5)

    def lo_body(c):
        i, u, n_ok, done = c
        u_try = u + bit(i)
        cnt = count_ge(lo_ref, u_try)
        take = jnp.where(done < 0.5, jnp.where(cnt >= need, 1.0, 0.0), 0.0) > 0.5
        return (i + 1, jnp.where(take, u_try, u), jnp.where(take, cnt, n_ok),
                jnp.where(cnt == need, 1.0, done))

    _, u_lo, n_ok, _ = lax.while_loop(
        lo_cond, lo_body,
        (jnp.int32(0), jnp.full((tq, LANES), -32768, I32), c_ok - c_gt,
         jnp.where(c_ok == k_f, 1.0, 0.0)))
    thr_all = _key_to_float(lax.shift_left(t_hi, 16) | (u_lo + 32768))
    thr = thr_all[:, 0:1]
    excess = jnp.where(thr == -jnp.inf, 0.0, (c_gt + n_ok - k_f)[:, 0:1])

    @pl.when(any_row(excess > 0.5))
    def _():
        def count_rows(pred):
            def body(j, a):
                s = sc_ref[:, pl.ds(tile_off(j), ts)]
                return a + _lane_fold(jnp.where(pred(s, j), 1.0, 0.0))
            a = lax.fori_loop(0, nkt, body, jnp.zeros((tq, LANES), F32))
            return jnp.sum(a, axis=1, keepdims=True)

        def tie_cond(c):
            jl, jh = c
            return any_row(jl + 1 < jh)

        def tie_body(c):
            jl, jh = c
            mid = lax.shift_right_arithmetic(jl + jh, 1)
            cnt = count_rows(lambda s, j: (s == thr) & (key_index(j) >= mid))
            ge = cnt >= excess
            return jnp.where(ge, mid, jl), jnp.where(ge, jh, mid)

        jl, _ = lax.while_loop(tie_cond, tie_body,
                               (jnp.zeros((tq, 1), I32), jnp.zeros((tq, 1), I32) + nkt * ts))

        def drop_body(j, carry):
            off = tile_off(j)
            s = sc_ref[:, pl.ds(off, ts)]
            drop = (s == thr) & (key_index(j) >= jl) & (excess > 0.5)
            sc_ref[:, pl.ds(off, ts)] = jnp.where(drop, -jnp.inf, s)
            return carry

        lax.fori_loop(0, nkt, drop_body, 0)

    thr_fin = lane_tile(jnp.maximum(thr_all, F32_LOWEST))

    def bias_body(j, carry):
        off = tile_off(j)
        s = sc_ref[:, pl.ds(off, ts)]
        sc_ref[:, pl.ds(off, ts)] = jnp.where(s >= thr_fin, 0.0, NEG_BIG)
        return carry

    lax.fori_loop(0, nkt, bias_body, 0)

    for h in range(N_HEADS):
        qs_ref[h] = q_ref[:, h * HEAD_DIM:(h + 1) * HEAD_DIM]
    m_ref[...] = jnp.full(m_ref.shape, NEG_BIG, F32)
    acc_ref[...] = jnp.zeros(acc_ref.shape, F32)

    def att_body(j, carry):
        off = tile_off(j)
        for rs in row_blocks:
            bias = sc_ref[rs, pl.ds(off, ts)]
            for h in range(N_HEADS):
                g = h // GROUP
                s = _dot(qs_ref[h, rs, :], kt_ref[g, :, pl.ds(off, ts)]) + bias
                m_old = m_ref[h, rs, :]
                m_new = jnp.maximum(m_old, jnp.max(s, axis=1, keepdims=True))
                p = jnp.exp(s - lane_tile(m_new)).astype(BF16)
                acc_ref[h, rs, :] = (jnp.exp(m_old - m_new) * acc_ref[h, rs, :]
                                     + _dot(p, v_ref[g, pl.ds(off, ts), :]))
                m_ref[h, rs, :] = m_new
        return carry

    lax.fori_loop(0, nkt, att_body, 0)
    for h in range(N_HEADS):
        acc = acc_ref[h]
        o_ref[:, h * HEAD_DIM:(h + 1) * HEAD_DIM] = (
            acc[:, 0:HEAD_DIM] / acc[:, HEAD_DIM:HEAD_DIM + 1])


def _dsa(q, qi, wi, kit, kt, v, *, pos0, s_real, tq, ts, rows, top_k):
    nb, t, nq = q.shape
    s_pad = kit.shape[2]
    tile3 = lambda w, dt: pltpu.VMEM((N_HEADS, tq, w), dt)
    assert s_pad % ts == 0 and t % tq == 0 and tq % rows == 0
    assert top_k <= ts and top_k <= s_real and s_pad // LANES <= 256
    kern = functools.partial(_dsa_kernel, pos0=pos0, s_real=s_real, ts=ts, rows=rows, top_k=top_k,
                             wi_scale=(IDX_HEADS * IDX_DIM) ** -0.5)
    return pl.pallas_call(
        kern,
        grid=(nb, t // tq),
        in_specs=[pl.BlockSpec((None, tq, nq), lambda b, i: (b, i, 0)),
                  pl.BlockSpec((None, tq, qi.shape[2]), lambda b, i: (b, i, 0)),
                  pl.BlockSpec((None, tq, wi.shape[2]), lambda b, i: (b, i, 0)),
                  pl.BlockSpec((None, IDX_DIM, s_pad), lambda b, i: (b, 0, 0)),
                  pl.BlockSpec((None, N_KV_HEADS, HEAD_DIM, s_pad), lambda b, i: (b, 0, 0, 0)),
                  pl.BlockSpec((None, N_KV_HEADS, s_pad, LANES), lambda b, i: (b, 0, 0, 0))],
        out_specs=pl.BlockSpec((None, tq, nq), lambda b, i: (b, i, 0)),
        out_shape=jax.ShapeDtypeStruct((nb, t, nq), F32),
        scratch_shapes=[pltpu.VMEM((tq, s_pad), F32), pltpu.VMEM((tq, s_pad), I16),
                        pltpu.VMEM((tq, s_pad), I16), tile3(IDX_DIM, BF16), tile3(LANES, F32),
                        tile3(HEAD_DIM, BF16), tile3(LANES, F32), tile3(LANES, F32)],
        compiler_params=pltpu.CompilerParams(dimension_semantics=("arbitrary", "arbitrary"),
                                             vmem_limit_bytes=VMEM_LIMIT),
        name="dsa",
    )(q, qi, wi, kit, kt, v)


def _tok_b_kernel(x1_ref, yc_ref, ya_ref, mk_ref, mv_ref, wmo_ref, g2_ref, b2_ref,
                  wq_ref, wo_ref, g3_ref, b3_ref, wg_ref, wu_ref, wd_ref, g4_ref, b4_ref,
                  y_ref, acc_ref, *, alpha, f_chunk):
    x1 = x1_ref[...]
    mixed = jnp.concatenate([yc_ref[...], ya_ref[...]], axis=-1).astype(BF16)
    x2 = _layer_norm(alpha * x1 + _dot(mixed, wmo_ref[...]), g2_ref[...], b2_ref[...])

    d = x2.shape[1]
    dh = d // MEM_HEADS
    qm = (_dot(x2.astype(BF16), wq_ref[...]) * (dh ** -0.5)).astype(BF16)
    mk = mk_ref[...].astype(BF16)
    mv = mv_ref[...].astype(BF16)
    heads = []
    for h in range(MEM_HEADS):
        sl = slice(h * dh, (h + 1) * dh)
        s = _dot_nt(qm[:, sl], mk[:, sl])
        e = jnp.exp(s - jnp.max(s, axis=-1, keepdims=True))
        p = e / jnp.sum(e, axis=-1, keepdims=True)
        heads.append(_dot(p.astype(BF16), mv[:, sl]))
    o = jnp.concatenate(heads, axis=-1).astype(BF16)
    x3 = _layer_norm(alpha * x2 + _dot(o, wo_ref[...]), g3_ref[...], b3_ref[...])

    _swiglu_into(acc_ref, x3.astype(BF16), wg_ref, wu_ref, wd_ref, f_chunk)
    y_ref[...] = _layer_norm(alpha * x3 + 0.5 * acc_ref[...], g4_ref[...], b4_ref[...])


def _tok_b(x1, yc, ya, mk, mv, wmo, g2, b2, wq, wo, g3, b3, wg, wu, wd, g4, b4,
           *, alpha, tm, seg_len):
    n, d = x1.shape
    n_mem = mk.shape[1]
    tiles_per_seg = seg_len // tm
    row = lambda w: pl.BlockSpec((tm, w), lambda i: (i, 0))
    mem = pl.BlockSpec((None, n_mem, d), lambda i: (i // tiles_per_seg, 0, 0))
    consts = [wmo, g2, b2, wq, wo, g3, b3, wg, wu, wd, g4, b4]
    kern = functools.partial(_tok_b_kernel, alpha=alpha, f_chunk=256)
    return pl.pallas_call(
        kern,
        grid=(n // tm,),
        in_specs=[row(d), row(yc.shape[1]), row(ya.shape[1]), mem, mem]
                 + [_const_spec(c.shape) for c in consts],
        out_specs=row(d),
        out_shape=jax.ShapeDtypeStruct((n, d), F32),
        scratch_shapes=[pltpu.VMEM((tm, d), F32)],
        compiler_params=pltpu.CompilerParams(dimension_semantics=("arbitrary",),
                                             vmem_limit_bytes=VMEM_LIMIT),
        name="tok_b",
    )(x1, yc, ya, mk, mv, *consts)


def _round_up(x, m):
    return (x + m - 1) // m * m


def _layer(x, conv_prev, k_past, v_past, ik_past, mem_k, mem_v, p, *, alpha, tm, tq, ts, rows):
    b, t, d = x.shape
    past = k_past.shape[1]
    d_conv = p["conv_w"].shape[1]
    top_k = min(TOPK_MAX, (past + t) // 4)

    cprev = jnp.pad(conv_prev, ((0, 0), (6, 0), (0, 0)))
    x1, yconv, q, k, v, qi, tail, ulast = _tok_a(
        x.reshape(b * t, d), cprev, p["ffn1_gate"], p["ffn1_up"], p["ffn1_down"],
        p["ln1_g"], p["ln1_b"], p["w_mix_in"], p["conv_w"], alpha=alpha, tm=tm, seg_len=t)
    new_conv = ulast.reshape(b, t // tm, 8, d_conv)[:, -1, 6:8, :]
    k = k.reshape(b, t, N_KV_HEADS, HEAD_DIM)
    v = v.reshape(b, t, N_KV_HEADS, HEAD_DIM)
    ki = tail[:, :IDX_DIM].reshape(b, t, IDX_DIM)
    wi = tail[:, IDX_DIM:IDX_DIM + IDX_HEADS].reshape(b, t, IDX_HEADS)

    s_real = past + t
    s_pad = _round_up(s_real, ts)
    pad = lambda a: jnp.pad(a, ((0, 0), (0, s_pad - s_real)) + ((0, 0),) * (a.ndim - 2))
    k_all = pad(jnp.concatenate([k_past, k], axis=1)).astype(BF16)
    v_all = pad(jnp.concatenate([v_past, v], axis=1)).astype(BF16)
    ki_all = pad(jnp.concatenate([ik_past, ki], axis=1)).astype(BF16)
    ones_col = (jnp.arange(LANES - HEAD_DIM) == 0).astype(BF16)
    v_aug = jnp.concatenate(
        [v_all.transpose(0, 2, 1, 3),
         jnp.broadcast_to(ones_col, (b, N_KV_HEADS, s_pad, LANES - HEAD_DIM))], axis=-1)
    y_attn = _dsa(q.reshape(b, t, -1), qi.reshape(b, t, -1), wi,
                  ki_all.transpose(0, 2, 1), k_all.transpose(0, 2, 3, 1), v_aug,
                  pos0=past, s_real=s_real, tq=tq, ts=ts, rows=rows, top_k=top_k)

    y = _tok_b(x1, yconv, y_attn.reshape(b * t, -1), mem_k.reshape(b, mem_k.shape[1], d),
               mem_v.reshape(b, mem_v.shape[1], d),
               p["w_mix_out"], p["ln2_g"], p["ln2_b"], p["w_mem_q"], p["w_mem_o"],
               p["ln3_g"], p["ln3_b"], p["ffn2_gate"], p["ffn2_up"], p["ffn2_down"],
               p["ln4_g"], p["ln4_b"], alpha=alpha, tm=tm, seg_len=t)
    return y.reshape(b, t, d), new_conv, k, v, ki


def kernel(x_prompt, x_sample, cache_conv, cache_k, cache_v, cache_idx_k, cache_mem_k, cache_mem_v,
           mem_prompt, ffn1_gate, ffn1_up, ffn1_down, ln1_g, ln1_b, w_mix_in, conv_w, w_mix_out,
           ln2_g, ln2_b, w_mem_q, w_mem_k, w_mem_v, w_mem_o, ln3_g, ln3_b,
           ffn2_gate, ffn2_up, ffn2_down, ln4_g, ln4_b):
    depth = ffn1_gate.shape[0]
    alpha = (2.0 * depth) ** 0.25
    b_p, t_p, d = x_prompt.shape
    dt = x_prompt.dtype
    d_conv = conv_w.shape[2]
    n_mem = mem_prompt.shape[1]
    bf = lambda w: w.astype(BF16)
    vec = lambda a: a.reshape(1, -1)

    y_p, y_s = x_prompt, x_sample
    outs_p = [[] for _ in range(6)]
    outs_s = [[] for _ in range(4)]
    for l in range(depth):
        d_in = w_mix_in.shape[2]
        p = dict(ffn1_gate=bf(ffn1_gate[l]), ffn1_up=bf(ffn1_up[l]), ffn1_down=bf(ffn1_down[l]),
                 ln1_g=vec(ln1_g[l]), ln1_b=vec(ln1_b[l]),
                 w_mix_in=jnp.pad(bf(w_mix_in[l]),
                                  ((0, 0), (0, _round_up(d_in, LANES) - d_in))),
                 conv_w=conv_w[l], w_mix_out=bf(w_mix_out[l]),
                 ln2_g=vec(ln2_g[l]), ln2_b=vec(ln2_b[l]),
                 w_mem_q=bf(w_mem_q[l]), w_mem_o=bf(w_mem_o[l]),
                 ln3_g=vec(ln3_g[l]), ln3_b=vec(ln3_b[l]),
                 ffn2_gate=bf(ffn2_gate[l]), ffn2_up=bf(ffn2_up[l]), ffn2_down=bf(ffn2_down[l]),
                 ln4_g=vec(ln4_g[l]), ln4_b=vec(ln4_b[l]))
        mem_kv = _memkv(mem_prompt.reshape(b_p * n_mem, d),
                        jnp.stack([bf(w_mem_k[l]), bf(w_mem_v[l])]))
        mem_k = mem_kv[0].reshape(b_p, n_mem, MEM_HEADS, d // MEM_HEADS)
        mem_v = mem_kv[1].reshape(b_p, n_mem, MEM_HEADS, d // MEM_HEADS)
        y_p, c_new, k_new, v_new, ik_new = _layer(
            y_p, jnp.zeros((b_p, 2, d_conv), dt),
            jnp.zeros((b_p, 0, N_KV_HEADS, HEAD_DIM), dt), jnp.zeros((b_p, 0, N_KV_HEADS, HEAD_DIM), dt),
            jnp.zeros((b_p, 0, IDX_DIM), dt), mem_k, mem_v, p,
            alpha=alpha, tm=512, tq=256, ts=256, rows=128)
        for lst, a in zip(outs_p, (c_new, k_new, v_new, ik_new, mem_k, mem_v)):
            lst.append(a)
        y_s, c_new, k_new, v_new, ik_new = _layer(
            y_s, cache_conv[l], cache_k[l], cache_v[l], cache_idx_k[l],
            cache_mem_k[l], cache_mem_v[l], p,
            alpha=alpha, tm=x_sample.shape[1], tq=x_sample.shape[1], ts=256, rows=x_sample.shape[1])
        for lst, a in zip(outs_s, (c_new, k_new, v_new, ik_new)):
            lst.append(a)
    return (y_p, y_s, *[jnp.stack(o) for o in outs_p], *[jnp.stack(o) for o in outs_s])
```

```python
import functools

import jax
import jax.numpy as jnp
from jax import lax
from jax.experimental import pallas as pl
from jax.experimental.pallas import tpu as pltpu

F32 = jnp.float32
BF16 = jnp.bfloat16
I32 = jnp.int32

CHUNK = 64
N_HEADS = 8
HEAD_DIM = 64
N_KV_HEADS = 2
GROUP = N_HEADS // N_KV_HEADS
IDX_HEADS = 8
IDX_DIM = 64
TOPK_MAX = 256
MEM_HEADS = 4
LN_EPS = 1e-5

LANES = 128
VMEM_LIMIT = 56 * 1024 * 1024
NEG_BIG = -1e30
F32_LOWEST = -3.4028234663852886e38
KEY_NEG_INF = -2139095041

TOKEN_TILE = 512
DSA_QUERY_TILE = 128
DSA_KEY_TILE = 256
FFN_CHUNK = 256


def _dot(a, b):
    return jnp.dot(a, b, preferred_element_type=F32)


def _dot_nt(a, b):
    return lax.dot_general(a, b, (((1,), (1,)), ((), ())), preferred_element_type=F32)


def _layer_norm(z, g, b):
    mu = jnp.mean(z, axis=-1, keepdims=True)
    d = z - mu
    var = jnp.mean(d * d, axis=-1, keepdims=True)
    return d * lax.rsqrt(var + LN_EPS) * g + b


def _swiglu_into(acc_ref, xb, wg_ref, wu_ref, wd_ref, f_chunk):
    d_ff = wg_ref.shape[1]
    for c in range(d_ff // f_chunk):
        sl = slice(c * f_chunk, (c + 1) * f_chunk)
        g = _dot(xb, wg_ref[:, sl])
        u = _dot(xb, wu_ref[:, sl])
        h = (g * (1.0 / (1.0 + jnp.exp(-g)))) * u
        part = _dot(h.astype(BF16), wd_ref[sl, :])
        if c == 0:
            acc_ref[...] = part
        else:
            acc_ref[...] += part


def _tok_a_kernel(x_ref, cprev_ref, wg_ref, wu_ref, wd_ref, g_ref, b_ref, win_ref, cw_ref,
                  x1_ref, yconv_ref, q_ref, k_ref, v_ref, qi_ref, tail_ref, ulast_ref,
                  acc_ref, ubuf_ref, *, alpha, tiles_per_seg, f_chunk, d_conv):
    i = pl.program_id(0)
    tm = x_ref.shape[0]
    x = x_ref[...]
    _swiglu_into(acc_ref, x.astype(BF16), wg_ref, wu_ref, wd_ref, f_chunk)
    x1 = _layer_norm(alpha * x + 0.5 * acc_ref[...], g_ref[...], b_ref[...])
    x1_ref[...] = x1
    x1b = x1.astype(BF16)

    c0 = 0
    h = _dot(x1b, win_ref[:, c0:c0 + d_conv]); c0 += d_conv
    gate_b = _dot(x1b, win_ref[:, c0:c0 + d_conv]); c0 += d_conv
    gate_c = _dot(x1b, win_ref[:, c0:c0 + d_conv]); c0 += d_conv
    nq = N_HEADS * HEAD_DIM
    q_ref[...] = (_dot(x1b, win_ref[:, c0:c0 + nq]) * (HEAD_DIM ** -0.5)).astype(BF16); c0 += nq
    nkv = N_KV_HEADS * HEAD_DIM
    k_ref[...] = _dot(x1b, win_ref[:, c0:c0 + nkv]); c0 += nkv
    v_ref[...] = _dot(x1b, win_ref[:, c0:c0 + nkv]); c0 += nkv
    nqi = IDX_HEADS * IDX_DIM
    qi_ref[...] = _dot(x1b, win_ref[:, c0:c0 + nqi]).astype(BF16); c0 += nqi
    tail_ref[...] = _dot(x1b, win_ref[:, c0:c0 + LANES])

    u = gate_c * h

    @pl.when(i % tiles_per_seg == 0)
    def _():
        ubuf_ref[0:8, :] = cprev_ref[0]

    ubuf_ref[8:tm + 8, :] = u
    um1 = ubuf_ref[7:tm + 7, :]
    um2 = ubuf_ref[6:tm + 6, :]
    cw = cw_ref[...]
    conv = cw[0:1, :] * um2 + cw[1:2, :] * um1 + cw[2:3, :] * u
    yconv_ref[...] = gate_b * conv
    last8 = ubuf_ref[tm:tm + 8, :]
    ulast_ref[0] = last8
    ubuf_ref[0:8, :] = last8


def _const_spec(shape):
    nd = len(shape)
    return pl.BlockSpec(shape, lambda *_: (0,) * nd, pipeline_mode=pl.Buffered(1))


def _tok_a(x, cprev, wg, wu, wd, g, b, win, cw, *, alpha, tm, seg_len):
    n, d = x.shape
    d_conv = cw.shape[1]
    n_tiles = n // tm
    row = lambda w: pl.BlockSpec((tm, w), lambda i: (i, 0))
    nq, nkv, nqi = N_HEADS * HEAD_DIM, N_KV_HEADS * HEAD_DIM, IDX_HEADS * IDX_DIM
    tiles_per_seg = seg_len // tm
    kern = functools.partial(_tok_a_kernel, alpha=alpha, tiles_per_seg=tiles_per_seg,
                             f_chunk=FFN_CHUNK, d_conv=d_conv)
    return pl.pallas_call(
        kern,
        grid=(n_tiles,),
        in_specs=[row(d),
                  pl.BlockSpec((1, 8, d_conv), lambda i: (i // tiles_per_seg, 0, 0)),
                  _const_spec(wg.shape), _const_spec(wu.shape), _const_spec(wd.shape),
                  _const_spec(g.shape), _const_spec(b.shape), _const_spec(win.shape),
                  _const_spec(cw.shape)],
        out_specs=[row(d), row(d_conv), row(nq), row(nkv), row(nkv), row(nqi), row(LANES),
                   pl.BlockSpec((1, 8, d_conv), lambda i: (i, 0, 0))],
        out_shape=[jax.ShapeDtypeStruct((n, d), F32),
                   jax.ShapeDtypeStruct((n, d_conv), F32),
                   jax.ShapeDtypeStruct((n, nq), BF16),
                   jax.ShapeDtypeStruct((n, nkv), F32),
                   jax.ShapeDtypeStruct((n, nkv), F32),
                   jax.ShapeDtypeStruct((n, nqi), BF16),
                   jax.ShapeDtypeStruct((n, LANES), F32),
                   jax.ShapeDtypeStruct((n_tiles, 8, d_conv), F32)],
        scratch_shapes=[pltpu.VMEM((tm, d), F32), pltpu.VMEM((tm + 8, d_conv), F32)],
        compiler_params=pltpu.CompilerParams(dimension_semantics=("arbitrary",),
                                             vmem_limit_bytes=VMEM_LIMIT),
        name="tok_a",
    )(x, cprev, wg, wu, wd, g, b, win, cw)


def _memkv_kernel(m_ref, w_ref, o_ref):
    o_ref[0] = _dot(m_ref[...].astype(BF16), w_ref[0])


def _memkv(mem, w2):
    n, d = mem.shape
    dout = w2.shape[2]
    return pl.pallas_call(
        _memkv_kernel,
        grid=(2,),
        in_specs=[pl.BlockSpec((n, d), lambda j: (0, 0)),
                  pl.BlockSpec((1, d, dout), lambda j: (j, 0, 0))],
        out_specs=pl.BlockSpec((1, n, dout), lambda j: (j, 0, 0)),
        out_shape=jax.ShapeDtypeStruct((2, n, dout), F32),
        compiler_params=pltpu.CompilerParams(dimension_semantics=("arbitrary",),
                                             vmem_limit_bytes=VMEM_LIMIT),
        name="memkv",
    )(mem, w2)


def _float_to_key(x):
    b = lax.bitcast_convert_type(x, I32)
    return jnp.where(b >= 0, b, b ^ 0x7FFFFFFF)


def _key_to_float(key):
    bits = jnp.where(key >= 0, key, key ^ 0x7FFFFFFF)
    return lax.bitcast_convert_type(bits, F32)


def _lane_fold(m, op):
    part = m[:, 0:LANES]
    for c in range(1, m.shape[1] // LANES):
        part = op(part, m[:, c * LANES:(c + 1) * LANES])
    return part


def _dsa_kernel(q_ref, qi_ref, wi_ref, kit_ref, kt_ref, v_ref, o_ref,
                sc_ref, rmax_ref, qis_ref, wb_ref, qs_ref, m_ref, acc_ref,
                *, pos0, s_real, ts, top_k, wi_scale):
    tq = q_ref.shape[0]
    q0 = pl.program_id(1) * tq
    n_adm = jnp.minimum(((pos0 + q0 + tq - 1) // CHUNK + 1) * CHUNK, s_real)
    nkt = (n_adm + ts - 1) // ts
    k_f = float(top_k)

    def tile_off(j):
        return pl.multiple_of(j * ts, ts)

    def key_index(j):
        return j * ts + lax.broadcasted_iota(I32, (1, ts), 1)

    def lane_tile(a):
        return jnp.concatenate([a] * (ts // LANES), axis=1)

    wi = wi_ref[...] * wi_scale
    for h in range(IDX_HEADS):
        qis_ref[h] = qi_ref[:, h * IDX_DIM:(h + 1) * IDX_DIM]
        wb_ref[h] = jnp.broadcast_to(wi[:, h:h + 1], (tq, LANES))
    q_chunk = lax.shift_right_arithmetic(
        pos0 + q0 + lax.broadcasted_iota(I32, (tq, 1), 0), 6)
    rmax_ref[...] = jnp.full(rmax_ref.shape, -jnp.inf, F32)

    def score_body(j, carry):
        off = tile_off(j)
        kit = kit_ref[:, pl.ds(off, ts)]
        acc = None
        for h in range(IDX_HEADS):
            term = lane_tile(wb_ref[h]) * jnp.maximum(_dot(qis_ref[h], kit), 0.0)
            acc = term if acc is None else acc + term
        kidx = key_index(j)
        admissible = (lax.shift_right_arithmetic(kidx, 6) <= q_chunk) & (kidx < s_real)
        s = jnp.where(admissible, acc, -jnp.inf)
        sc_ref[:, pl.ds(off, ts)] = s
        rmax_ref[...] = jnp.maximum(rmax_ref[...], _lane_fold(s, jnp.maximum))
        return carry

    lax.fori_loop(0, nkt, score_body, 0)

    def count_rows(pred):
        def body(j, a):
            s = sc_ref[:, pl.ds(tile_off(j), ts)]
            return a + _lane_fold(jnp.where(pred(s, j), 1.0, 0.0), jnp.add)
        a = lax.fori_loop(0, nkt, body, jnp.zeros((tq, LANES), F32))
        return jnp.sum(a, axis=1, keepdims=True)

    def any_row(flag):
        return jnp.max(jnp.where(flag, 1.0, 0.0)) > 0.5

    def key_mid(lo, hi):
        return (lax.shift_right_arithmetic(lo, 1) + lax.shift_right_arithmetic(hi, 1)
                + (lo & hi & 1))

    def probe(lo, hi, cnt_lo, mid):
        thr = _key_to_float(mid)
        cnt = count_rows(lambda s, j: s >= thr)
        ge = cnt >= k_f
        exact = cnt == k_f
        lo_n = jnp.where(ge, mid, lo)
        hi_n = jnp.where(exact, mid + 1, jnp.where(ge, hi, mid))
        return lo_n, hi_n, jnp.where(ge, cnt, cnt_lo)

    row_max = jnp.max(rmax_ref[...], axis=1, keepdims=True)
    lo0 = jnp.full((tq, 1), KEY_NEG_INF, I32)
    hi0 = _float_to_key(row_max) + 1
    cnt0 = jnp.zeros((tq, 1), F32) + (nkt * ts).astype(F32)
    mid0 = jnp.where(row_max > 0.0, _float_to_key(row_max * 0.25), key_mid(lo0, hi0))
    lo, _, cnt_lo = lax.while_loop(lambda c: any_row(c[0] + 1 < c[1]),
                                   lambda c: probe(*c, key_mid(c[0], c[1])),
                                   probe(lo0, hi0, cnt0, mid0))
    thr = _key_to_float(lo)
    excess = jnp.where(thr == -jnp.inf, 0.0, cnt_lo - k_f)

    @pl.when(any_row(excess > 0.5))
    def _():
        def tie_cond(c):
            jl, jh = c
            return any_row(jl + 1 < jh)

        def tie_body(c):
            jl, jh = c
            mid = lax.shift_right_arithmetic(jl + jh, 1)
            cnt = count_rows(lambda s, j: (s == thr) & (key_index(j) >= mid))
            ge = cnt >= excess
            return jnp.where(ge, mid, jl), jnp.where(ge, jh, mid)

        jl, _ = lax.while_loop(tie_cond, tie_body,
                               (jnp.zeros((tq, 1), I32), jnp.zeros((tq, 1), I32) + nkt * ts))

        def drop_body(j, carry):
            off = tile_off(j)
            s = sc_ref[:, pl.ds(off, ts)]
            drop = (s == thr) & (key_index(j) >= jl) & (excess > 0.5)
            sc_ref[:, pl.ds(off, ts)] = jnp.where(drop, -jnp.inf, s)
            return carry

        lax.fori_loop(0, nkt, drop_body, 0)

    thr_fin = jnp.maximum(thr, F32_LOWEST)

    def bias_body(j, carry):
        off = tile_off(j)
        s = sc_ref[:, pl.ds(off, ts)]
        sc_ref[:, pl.ds(off, ts)] = jnp.where(s >= thr_fin, 0.0, NEG_BIG)
        return carry

    lax.fori_loop(0, nkt, bias_body, 0)

    for h in range(N_HEADS):
        qs_ref[h] = q_ref[:, h * HEAD_DIM:(h + 1) * HEAD_DIM]
    m_ref[...] = jnp.full(m_ref.shape, NEG_BIG, F32)
    acc_ref[...] = jnp.zeros(acc_ref.shape, F32)

    def att_body(j, carry):
        off = tile_off(j)
        bias = sc_ref[:, pl.ds(off, ts)]
        for h in range(N_HEADS):
            g = h // GROUP
            s = _dot(qs_ref[h], kt_ref[g, :, pl.ds(off, ts)]) + bias
            m_old = m_ref[h]
            m_new = jnp.maximum(m_old, jnp.max(s, axis=1, keepdims=True))
            p = jnp.exp(s - lane_tile(m_new)).astype(BF16)
            acc_ref[h] = (jnp.exp(m_old - m_new) * acc_ref[h]
                          + _dot(p, v_ref[g, pl.ds(off, ts), :]))
            m_ref[h] = m_new
        return carry

    lax.fori_loop(0, nkt, att_body, 0)
    for h in range(N_HEADS):
        acc = acc_ref[h]
        o_ref[:, h * HEAD_DIM:(h + 1) * HEAD_DIM] = (
            acc[:, 0:HEAD_DIM] / acc[:, HEAD_DIM:HEAD_DIM + 1])


def _dsa(q, qi, wi, kit, kt, v, *, pos0, s_real, tq, ts, top_k):
    nb, t, nq = q.shape
    s_pad = kit.shape[2]
    tile3 = lambda w, dt: pltpu.VMEM((N_HEADS, tq, w), dt)
    assert s_pad % ts == 0 and t % tq == 0 and top_k <= ts and top_k <= s_real
    kern = functools.partial(_dsa_kernel, pos0=pos0, s_real=s_real, ts=ts, top_k=top_k,
                             wi_scale=(IDX_HEADS * IDX_DIM) ** -0.5)
    return pl.pallas_call(
        kern,
        grid=(nb, t // tq),
        in_specs=[pl.BlockSpec((None, tq, nq), lambda b, i: (b, i, 0)),
                  pl.BlockSpec((None, tq, qi.shape[2]), lambda b, i: (b, i, 0)),
                  pl.BlockSpec((None, tq, wi.shape[2]), lambda b, i: (b, i, 0)),
                  pl.BlockSpec((None, IDX_DIM, s_pad), lambda b, i: (b, 0, 0)),
                  pl.BlockSpec((None, N_KV_HEADS, HEAD_DIM, s_pad), lambda b, i: (b, 0, 0, 0)),
                  pl.BlockSpec((None, N_KV_HEADS, s_pad, LANES), lambda b, i: (b, 0, 0, 0))],
        out_specs=pl.BlockSpec((None, tq, nq), lambda b, i: (b, i, 0)),
        out_shape=jax.ShapeDtypeStruct((nb, t, nq), F32),
        scratch_shapes=[pltpu.VMEM((tq, s_pad), F32), pltpu.VMEM((tq, LANES), F32),
                        tile3(IDX_DIM, BF16), tile3(LANES, F32),
                        tile3(HEAD_DIM, BF16), tile3(LANES, F32), tile3(LANES, F32)],
        compiler_params=pltpu.CompilerParams(dimension_semantics=("arbitrary", "arbitrary"),
                                             vmem_limit_bytes=VMEM_LIMIT),
        name="dsa",
    )(q, qi, wi, kit, kt, v)


def _tok_b_kernel(x1_ref, yc_ref, ya_ref, mk_ref, mv_ref, wmo_ref, g2_ref, b2_ref,
                  wq_ref, wo_ref, g3_ref, b3_ref, wg_ref, wu_ref, wd_ref, g4_ref, b4_ref,
                  y_ref, acc_ref, *, alpha, f_chunk):
    x1 = x1_ref[...]
    mixed = jnp.concatenate([yc_ref[...], ya_ref[...]], axis=-1).astype(BF16)
    x2 = _layer_norm(alpha * x1 + _dot(mixed, wmo_ref[...]), g2_ref[...], b2_ref[...])

    d = x2.shape[1]
    dh = d // MEM_HEADS
    qm = (_dot(x2.astype(BF16), wq_ref[...]) * (dh ** -0.5)).astype(BF16)
    mk = mk_ref[...].astype(BF16)
    mv = mv_ref[...].astype(BF16)
    heads = []
    for h in range(MEM_HEADS):
        sl = slice(h * dh, (h + 1) * dh)
        s = _dot_nt(qm[:, sl], mk[:, sl])
        e = jnp.exp(s - jnp.max(s, axis=-1, keepdims=True))
        p = e / jnp.sum(e, axis=-1, keepdims=True)
        heads.append(_dot(p.astype(BF16), mv[:, sl]))
    o = jnp.concatenate(heads, axis=-1).astype(BF16)
    x3 = _layer_norm(alpha * x2 + _dot(o, wo_ref[...]), g3_ref[...], b3_ref[...])

    _swiglu_into(acc_ref, x3.astype(BF16), wg_ref, wu_ref, wd_ref, f_chunk)
    y_ref[...] = _layer_norm(alpha * x3 + 0.5 * acc_ref[...], g4_ref[...], b4_ref[...])


def _tok_b(x1, yc, ya, mk, mv, wmo, g2, b2, wq, wo, g3, b3, wg, wu, wd, g4, b4,
           *, alpha, tm, seg_len):
    n, d = x1.shape
    n_mem = mk.shape[1]
    tiles_per_seg = seg_len // tm
    row = lambda w: pl.BlockSpec((tm, w), lambda i: (i, 0))
    mem = pl.BlockSpec((None, n_mem, d), lambda i: (i // tiles_per_seg, 0, 0))
    consts = [wmo, g2, b2, wq, wo, g3, b3, wg, wu, wd, g4, b4]
    kern = functools.partial(_tok_b_kernel, alpha=alpha, f_chunk=FFN_CHUNK)
    return pl.pallas_call(
        kern,
        grid=(n // tm,),
        in_specs=[row(d), row(yc.shape[1]), row(ya.shape[1]), mem, mem]
                 + [_const_spec(c.shape) for c in consts],
        out_specs=row(d),
        out_shape=jax.ShapeDtypeStruct((n, d), F32),
        scratch_shapes=[pltpu.VMEM((tm, d), F32)],
        compiler_params=pltpu.CompilerParams(dimension_semantics=("arbitrary",),
                                             vmem_limit_bytes=VMEM_LIMIT),
        name="tok_b",
    )(x1, yc, ya, mk, mv, *consts)


def _round_up(x, m):
    return (x + m - 1) // m * m


def _layer(x, conv_prev, k_past, v_past, ik_past, mem_k, mem_v, p, *, alpha):
    b, t, d = x.shape
    past = k_past.shape[1]
    d_conv = p["conv_w"].shape[1]
    top_k = min(TOPK_MAX, (past + t) // 4)
    tm = min(TOKEN_TILE, t)
    tq = min(DSA_QUERY_TILE, t)
    ts = DSA_KEY_TILE

    cprev = jnp.pad(conv_prev, ((0, 0), (6, 0), (0, 0)))
    x1, yconv, q, k, v, qi, tail, ulast = _tok_a(
        x.reshape(b * t, d), cprev, p["ffn1_gate"], p["ffn1_up"], p["ffn1_down"],
        p["ln1_g"], p["ln1_b"], p["w_mix_in"], p["conv_w"], alpha=alpha, tm=tm, seg_len=t)
    new_conv = ulast.reshape(b, t // tm, 8, d_conv)[:, -1, 6:8, :]
    k = k.reshape(b, t, N_KV_HEADS, HEAD_DIM)
    v = v.reshape(b, t, N_KV_HEADS, HEAD_DIM)
    ki = tail[:, :IDX_DIM].reshape(b, t, IDX_DIM)
    wi = tail[:, IDX_DIM:IDX_DIM + IDX_HEADS].reshape(b, t, IDX_HEADS)

    s_real = past + t
    s_pad = _round_up(s_real, ts)
    pad = lambda a: jnp.pad(a, ((0, 0), (0, s_pad - s_real)) + ((0, 0),) * (a.ndim - 2))
    k_all = pad(jnp.concatenate([k_past, k], axis=1)).astype(BF16)
    v_all = pad(jnp.concatenate([v_past, v], axis=1)).astype(BF16)
    ki_all = pad(jnp.concatenate([ik_past, ki], axis=1)).astype(BF16)
    ones_col = (jnp.arange(LANES - HEAD_DIM) == 0).astype(BF16)
    v_aug = jnp.concatenate(
        [v_all.transpose(0, 2, 1, 3),
         jnp.broadcast_to(ones_col, (b, N_KV_HEADS, s_pad, LANES - HEAD_DIM))], axis=-1)
    y_attn = _dsa(q.reshape(b, t, -1), qi.reshape(b, t, -1), wi,
                  ki_all.transpose(0, 2, 1), k_all.transpose(0, 2, 3, 1), v_aug,
                  pos0=past, s_real=s_real, tq=tq, ts=ts, top_k=top_k)

    y = _tok_b(x1, yconv, y_attn.reshape(b * t, -1), mem_k.reshape(b, mem_k.shape[1], d),
               mem_v.reshape(b, mem_v.shape[1], d),
               p["w_mix_out"], p["ln2_g"], p["ln2_b"], p["w_mem_q"], p["w_mem_o"],
               p["ln3_g"], p["ln3_b"], p["ffn2_gate"], p["ffn2_up"], p["ffn2_down"],
               p["ln4_g"], p["ln4_b"], alpha=alpha, tm=tm, seg_len=t)
    return y.reshape(b, t, d), new_conv, k, v, ki


def kernel(x_prompt, x_sample, cache_conv, cache_k, cache_v, cache_idx_k, cache_mem_k, cache_mem_v,
           mem_prompt, ffn1_gate, ffn1_up, ffn1_down, ln1_g, ln1_b, w_mix_in, conv_w, w_mix_out,
           ln2_g, ln2_b, w_mem_q, w_mem_k, w_mem_v, w_mem_o, ln3_g, ln3_b,
           ffn2_gate, ffn2_up, ffn2_down, ln4_g, ln4_b):
    depth = ffn1_gate.shape[0]
    alpha = (2.0 * depth) ** 0.25
    b_p, t_p, d = x_prompt.shape
    dt = x_prompt.dtype
    d_conv = conv_w.shape[2]
    n_mem = mem_prompt.shape[1]
    bf = lambda w: w.astype(BF16)
    vec = lambda a: a.reshape(1, -1)

    y_p, y_s = x_prompt, x_sample
    outs_p = [[] for _ in range(6)]
    outs_s = [[] for _ in range(4)]
    for l in range(depth):
        d_in = w_mix_in.shape[2]
        p = dict(ffn1_gate=bf(ffn1_gate[l]), ffn1_up=bf(ffn1_up[l]), ffn1_down=bf(ffn1_down[l]),
                 ln1_g=vec(ln1_g[l]), ln1_b=vec(ln1_b[l]),
                 w_mix_in=jnp.pad(bf(w_mix_in[l]),
                                  ((0, 0), (0, _round_up(d_in, LANES) - d_in))),
                 conv_w=conv_w[l], w_mix_out=bf(w_mix_out[l]),
                 ln2_g=vec(ln2_g[l]), ln2_b=vec(ln2_b[l]),
                 w_mem_q=bf(w_mem_q[l]), w_mem_o=bf(w_mem_o[l]),
                 ln3_g=vec(ln3_g[l]), ln3_b=vec(ln3_b[l]),
                 ffn2_gate=bf(ffn2_gate[l]), ffn2_up=bf(ffn2_up[l]), ffn2_down=bf(ffn2_down[l]),
                 ln4_g=vec(ln4_g[l]), ln4_b=vec(ln4_b[l]))
        mem_kv = _memkv(mem_prompt.reshape(b_p * n_mem, d),
                        jnp.stack([bf(w_mem_k[l]), bf(w_mem_v[l])]))
        mem_k = mem_kv[0].reshape(b_p, n_mem, MEM_HEADS, d // MEM_HEADS)
        mem_v = mem_kv[1].reshape(b_p, n_mem, MEM_HEADS, d // MEM_HEADS)
        y_p, c_new, k_new, v_new, ik_new = _layer(
            y_p, jnp.zeros((b_p, 2, d_conv), dt),
            jnp.zeros((b_p, 0, N_KV_HEADS, HEAD_DIM), dt), jnp.zeros((b_p, 0, N_KV_HEADS, HEAD_DIM), dt),
            jnp.zeros((b_p, 0, IDX_DIM), dt), mem_k, mem_v, p, alpha=alpha)
        for lst, a in zip(outs_p, (c_new, k_new, v_new, ik_new, mem_k, mem_v)):
            lst.append(a)
        y_s, c_new, k_new, v_new, ik_new = _layer(
            y_s, cache_conv[l], cache_k[l], cache_v[l], cache_idx_k[l],
            cache_mem_k[l], cache_mem_v[l], p, alpha=alpha)
        for lst, a in zip(outs_s, (c_new, k_new, v_new, ik_new)):
            lst.append(a)
    return (y_p, y_s, *[jnp.stack(o) for o in outs_p], *[jnp.stack(o) for o in outs_s])
```

```python
import functools

import jax
import jax.numpy as jnp
from jax import lax
from jax.experimental import pallas as pl
from jax.experimental.pallas import tpu as pltpu

F32 = jnp.float32
BF16 = jnp.bfloat16
I32 = jnp.int32

CHUNK = 64
N_HEADS = 8
HEAD_DIM = 64
N_KV_HEADS = 2
GROUP = N_HEADS // N_KV_HEADS
IDX_HEADS = 8
IDX_DIM = 64
TOPK_MAX = 256
MEM_HEADS = 4
LN_EPS = 1e-5

LANES = 128
VMEM_LIMIT = 56 * 1024 * 1024
NEG_BIG = -1e30
F32_LOWEST = -3.4028234663852886e38
KEY_NEG_INF = -2139095041

TOKEN_TILE = 512
DSA_SELECT_ROWS = 128
DSA_ATTEND_ROWS = 512
DSA_ATTEND_BLOCK = 128
DSA_KEY_TILE = 256
FFN_CHUNK = 256


def _dot(a, b):
    return jnp.dot(a, b, preferred_element_type=F32)


def _dot_nt(a, b):
    return lax.dot_general(a, b, (((1,), (1,)), ((), ())), preferred_element_type=F32)


def _layer_norm(z, g, b):
    mu = jnp.mean(z, axis=-1, keepdims=True)
    d = z - mu
    var = jnp.mean(d * d, axis=-1, keepdims=True)
    return d * lax.rsqrt(var + LN_EPS) * g + b


def _swiglu_into(acc_ref, xb, wg_ref, wu_ref, wd_ref, f_chunk):
    d_ff = wg_ref.shape[1]
    for c in range(d_ff // f_chunk):
        sl = slice(c * f_chunk, (c + 1) * f_chunk)
        g = _dot(xb, wg_ref[:, sl])
        u = _dot(xb, wu_ref[:, sl])
        h = (g * (1.0 / (1.0 + jnp.exp(-g)))) * u
        part = _dot(h.astype(BF16), wd_ref[sl, :])
        if c == 0:
            acc_ref[...] = part
        else:
            acc_ref[...] += part


def _tok_a_kernel(x_ref, cprev_ref, wg_ref, wu_ref, wd_ref, g_ref, b_ref, win_ref, cw_ref,
                  x1_ref, yconv_ref, q_ref, k_ref, v_ref, qi_ref, tail_ref, ulast_ref,
                  acc_ref, ubuf_ref, *, alpha, tiles_per_seg, f_chunk, d_conv):
    i = pl.program_id(0)
    tm = x_ref.shape[0]
    x = x_ref[...]
    _swiglu_into(acc_ref, x.astype(BF16), wg_ref, wu_ref, wd_ref, f_chunk)
    x1 = _layer_norm(alpha * x + 0.5 * acc_ref[...], g_ref[...], b_ref[...])
    x1_ref[...] = x1
    x1b = x1.astype(BF16)

    c0 = 0
    h = _dot(x1b, win_ref[:, c0:c0 + d_conv]); c0 += d_conv
    gate_b = _dot(x1b, win_ref[:, c0:c0 + d_conv]); c0 += d_conv
    gate_c = _dot(x1b, win_ref[:, c0:c0 + d_conv]); c0 += d_conv
    nq = N_HEADS * HEAD_DIM
    q_ref[...] = (_dot(x1b, win_ref[:, c0:c0 + nq]) * (HEAD_DIM ** -0.5)).astype(BF16); c0 += nq
    nkv = N_KV_HEADS * HEAD_DIM
    k_ref[...] = _dot(x1b, win_ref[:, c0:c0 + nkv]); c0 += nkv
    v_ref[...] = _dot(x1b, win_ref[:, c0:c0 + nkv]); c0 += nkv
    nqi = IDX_HEADS * IDX_DIM
    qi_ref[...] = _dot(x1b, win_ref[:, c0:c0 + nqi]).astype(BF16); c0 += nqi
    tail_ref[...] = _dot(x1b, win_ref[:, c0:c0 + LANES])

    u = gate_c * h

    @pl.when(i % tiles_per_seg == 0)
    def _():
        ubuf_ref[0:8, :] = cprev_ref[0]

    ubuf_ref[8:tm + 8, :] = u
    um1 = ubuf_ref[7:tm + 7, :]
    um2 = ubuf_ref[6:tm + 6, :]
    cw = cw_ref[...]
    conv = cw[0:1, :] * um2 + cw[1:2, :] * um1 + cw[2:3, :] * u
    yconv_ref[...] = gate_b * conv
    last8 = ubuf_ref[tm:tm + 8, :]
    ulast_ref[0] = last8
    ubuf_ref[0:8, :] = last8


def _const_spec(shape):
    nd = len(shape)
    return pl.BlockSpec(shape, lambda *_: (0,) * nd, pipeline_mode=pl.Buffered(1))


def _tok_a(x, cprev, wg, wu, wd, g, b, win, cw, *, alpha, tm, seg_len):
    n, d = x.shape
    d_conv = cw.shape[1]
    n_tiles = n // tm
    row = lambda w: pl.BlockSpec((tm, w), lambda i: (i, 0))
    nq, nkv, nqi = N_HEADS * HEAD_DIM, N_KV_HEADS * HEAD_DIM, IDX_HEADS * IDX_DIM
    tiles_per_seg = seg_len // tm
    kern = functools.partial(_tok_a_kernel, alpha=alpha, tiles_per_seg=tiles_per_seg,
                             f_chunk=FFN_CHUNK, d_conv=d_conv)
    return pl.pallas_call(
        kern,
        grid=(n_tiles,),
        in_specs=[row(d),
                  pl.BlockSpec((1, 8, d_conv), lambda i: (i // tiles_per_seg, 0, 0)),
                  _const_spec(wg.shape), _const_spec(wu.shape), _const_spec(wd.shape),
                  _const_spec(g.shape), _const_spec(b.shape), _const_spec(win.shape),
                  _const_spec(cw.shape)],
        out_specs=[row(d), row(d_conv), row(nq), row(nkv), row(nkv), row(nqi), row(LANES),
                   pl.BlockSpec((1, 8, d_conv), lambda i: (i, 0, 0))],
        out_shape=[jax.ShapeDtypeStruct((n, d), F32),
                   jax.ShapeDtypeStruct((n, d_conv), F32),
                   jax.ShapeDtypeStruct((n, nq), BF16),
                   jax.ShapeDtypeStruct((n, nkv), F32),
                   jax.ShapeDtypeStruct((n, nkv), F32),
                   jax.ShapeDtypeStruct((n, nqi), BF16),
                   jax.ShapeDtypeStruct((n, LANES), F32),
                   jax.ShapeDtypeStruct((n_tiles, 8, d_conv), F32)],
        scratch_shapes=[pltpu.VMEM((tm, d), F32), pltpu.VMEM((tm + 8, d_conv), F32)],
        compiler_params=pltpu.CompilerParams(dimension_semantics=("arbitrary",),
                                             vmem_limit_bytes=VMEM_LIMIT),
        name="tok_a",
    )(x, cprev, wg, wu, wd, g, b, win, cw)


def _memkv_kernel(m_ref, w_ref, o_ref):
    o_ref[0] = _dot(m_ref[...].astype(BF16), w_ref[0])


def _memkv(mem, w2):
    n, d = mem.shape
    dout = w2.shape[2]
    return pl.pallas_call(
        _memkv_kernel,
        grid=(2,),
        in_specs=[pl.BlockSpec((n, d), lambda j: (0, 0)),
                  pl.BlockSpec((1, d, dout), lambda j: (j, 0, 0))],
        out_specs=pl.BlockSpec((1, n, dout), lambda j: (j, 0, 0)),
        out_shape=jax.ShapeDtypeStruct((2, n, dout), F32),
        compiler_params=pltpu.CompilerParams(dimension_semantics=("arbitrary",),
                                             vmem_limit_bytes=VMEM_LIMIT),
        name="memkv",
    )(mem, w2)


def _float_to_key(x):
    b = lax.bitcast_convert_type(x, I32)
    return jnp.where(b >= 0, b, b ^ 0x7FFFFFFF)


def _key_to_float(key):
    bits = jnp.where(key >= 0, key, key ^ 0x7FFFFFFF)
    return lax.bitcast_convert_type(bits, F32)


def _lane_fold(m, op):
    part = m[:, 0:LANES]
    for c in range(1, m.shape[1] // LANES):
        part = op(part, m[:, c * LANES:(c + 1) * LANES])
    return part


def _admissible_keys(pos0, q0, tq, s_real):
    return jnp.minimum(((pos0 + q0 + tq - 1) // CHUNK + 1) * CHUNK, s_real)


def _dsa_select_kernel(qi_ref, wi_ref, kit_ref, sc_ref, rmax_ref, qis_ref, wb_ref,
                       *, pos0, s_real, ts, top_k, wi_scale):
    tq = qi_ref.shape[0]
    q0 = pl.program_id(1) * tq
    nkt = (_admissible_keys(pos0, q0, tq, s_real) + ts - 1) // ts
    k_f = float(top_k)

    def tile_off(j):
        return pl.multiple_of(j * ts, ts)

    def key_index(j):
        return j * ts + lax.broadcasted_iota(I32, (1, ts), 1)

    def lane_tile(a):
        return jnp.concatenate([a] * (ts // LANES), axis=1)

    wi = wi_ref[...] * wi_scale
    for h in range(IDX_HEADS):
        qis_ref[h] = qi_ref[:, h * IDX_DIM:(h + 1) * IDX_DIM]
        wb_ref[h] = jnp.broadcast_to(wi[:, h:h + 1], (tq, LANES))
    q_chunk = lax.shift_right_arithmetic(
        pos0 + q0 + lax.broadcasted_iota(I32, (tq, 1), 0), 6)
    rmax_ref[...] = jnp.full(rmax_ref.shape, -jnp.inf, F32)

    def score_body(j, carry):
        off = tile_off(j)
        kit = kit_ref[:, pl.ds(off, ts)]
        acc = None
        for h in range(IDX_HEADS):
            term = lane_tile(wb_ref[h]) * jnp.maximum(_dot(qis_ref[h], kit), 0.0)
            acc = term if acc is None else acc + term
        kidx = key_index(j)
        admissible = (lax.shift_right_arithmetic(kidx, 6) <= q_chunk) & (kidx < s_real)
        s = jnp.where(admissible, acc, -jnp.inf)
        sc_ref[:, pl.ds(off, ts)] = s
        rmax_ref[...] = jnp.maximum(rmax_ref[...], _lane_fold(s, jnp.maximum))
        return carry

    lax.fori_loop(0, nkt, score_body, 0)

    def count_rows(pred):
        def body(j, a):
            s = sc_ref[:, pl.ds(tile_off(j), ts)]
            return a + _lane_fold(jnp.where(pred(s, j), 1.0, 0.0), jnp.add)
        a = lax.fori_loop(0, nkt, body, jnp.zeros((tq, LANES), F32))
        return jnp.sum(a, axis=1, keepdims=True)

    def any_row(flag):
        return jnp.max(jnp.where(flag, 1.0, 0.0)) > 0.5

    def key_mid(lo, hi):
        return (lax.shift_right_arithmetic(lo, 1) + lax.shift_right_arithmetic(hi, 1)
                + (lo & hi & 1))

    def probe(lo, hi, cnt_lo, mid):
        thr = _key_to_float(mid)
        cnt = count_rows(lambda s, j: s >= thr)
        ge = cnt >= k_f
        exact = cnt == k_f
        lo_n = jnp.where(ge, mid, lo)
        hi_n = jnp.where(exact, mid + 1, jnp.where(ge, hi, mid))
        return lo_n, hi_n, jnp.where(ge, cnt, cnt_lo)

    row_max = jnp.max(rmax_ref[...], axis=1, keepdims=True)
    lo0 = jnp.full((tq, 1), KEY_NEG_INF, I32)
    hi0 = _float_to_key(row_max) + 1
    cnt0 = jnp.zeros((tq, 1), F32) + (nkt * ts).astype(F32)
    mid0 = jnp.where(row_max > 0.0, _float_to_key(row_max * 0.25), key_mid(lo0, hi0))
    lo, _, cnt_lo = lax.while_loop(lambda c: any_row(c[0] + 1 < c[1]),
                                   lambda c: probe(*c, key_mid(c[0], c[1])),
                                   probe(lo0, hi0, cnt0, mid0))
    thr = _key_to_float(lo)
    excess = jnp.where(thr == -jnp.inf, 0.0, cnt_lo - k_f)

    @pl.when(any_row(excess > 0.5))
    def _():
        def tie_cond(c):
            jl, jh = c
            return any_row(jl + 1 < jh)

        def tie_body(c):
            jl, jh = c
            mid = lax.shift_right_arithmetic(jl + jh, 1)
            cnt = count_rows(lambda s, j: (s == thr) & (key_index(j) >= mid))
            ge = cnt >= excess
            return jnp.where(ge, mid, jl), jnp.where(ge, jh, mid)

        jl, _ = lax.while_loop(tie_cond, tie_body,
                               (jnp.zeros((tq, 1), I32), jnp.zeros((tq, 1), I32) + nkt * ts))

        def drop_body(j, carry):
            off = tile_off(j)
            s = sc_ref[:, pl.ds(off, ts)]
            drop = (s == thr) & (key_index(j) >= jl) & (excess > 0.5)
            sc_ref[:, pl.ds(off, ts)] = jnp.where(drop, -jnp.inf, s)
            return carry

        lax.fori_loop(0, nkt, drop_body, 0)

    thr_fin = jnp.maximum(thr, F32_LOWEST)

    def bias_body(j, carry):
        off = tile_off(j)
        s = sc_ref[:, pl.ds(off, ts)]
        sc_ref[:, pl.ds(off, ts)] = jnp.where(s >= thr_fin, 0.0, NEG_BIG)
        return carry

    lax.fori_loop(0, nkt, bias_body, 0)

    def fill_body(j, carry):
        sc_ref[:, pl.ds(tile_off(j), ts)] = jnp.full((tq, ts), NEG_BIG, F32)
        return carry

    lax.fori_loop(nkt, sc_ref.shape[1] // ts, fill_body, 0)


def _dsa_attend_kernel(q_ref, bias_ref, kt_ref, v_ref, o_ref, qs_ref, m_ref, acc_ref,
                       *, pos0, s_real, ts, rows):
    tq = q_ref.shape[0]
    row_blocks = [slice(r * rows, (r + 1) * rows) for r in range(tq // rows)]
    q0 = pl.program_id(1) * tq
    nkt = (_admissible_keys(pos0, q0, tq, s_real) + ts - 1) // ts

    def lane_tile(a):
        return jnp.concatenate([a] * (ts // LANES), axis=1)

    for h in range(N_HEADS):
        qs_ref[h] = q_ref[:, h * HEAD_DIM:(h + 1) * HEAD_DIM]
    m_ref[...] = jnp.full(m_ref.shape, NEG_BIG, F32)
    acc_ref[...] = jnp.zeros(acc_ref.shape, F32)

    def att_body(j, carry):
        off = pl.multiple_of(j * ts, ts)
        for rs in row_blocks:
            bias = bias_ref[rs, pl.ds(off, ts)]
            for h in range(N_HEADS):
                g = h // GROUP
                s = _dot(qs_ref[h, rs, :], kt_ref[g, :, pl.ds(off, ts)]) + bias
                m_old = m_ref[h, rs, :]
                m_new = jnp.maximum(m_old, jnp.max(s, axis=1, keepdims=True))
                p = jnp.exp(s - lane_tile(m_new)).astype(BF16)
                acc_ref[h, rs, :] = (jnp.exp(m_old - m_new) * acc_ref[h, rs, :]
                                     + _dot(p, v_ref[g, pl.ds(off, ts), :]))
                m_ref[h, rs, :] = m_new
        return carry

    lax.fori_loop(0, nkt, att_body, 0)
    for h in range(N_HEADS):
        acc = acc_ref[h]
        o_ref[:, h * HEAD_DIM:(h + 1) * HEAD_DIM] = (
            acc[:, 0:HEAD_DIM] / acc[:, HEAD_DIM:HEAD_DIM + 1])


def _dsa(q, qi, wi, kit, kt, v, *, pos0, s_real, tq_sel, tq_att, rows, ts, top_k):
    nb, t, nq = q.shape
    s_pad = kit.shape[2]
    assert s_pad % ts == 0 and t % tq_sel == 0 and t % tq_att == 0 and tq_att % rows == 0
    assert top_k <= ts and top_k <= s_real
    params = pltpu.CompilerParams(dimension_semantics=("arbitrary", "arbitrary"),
                                  vmem_limit_bytes=VMEM_LIMIT)
    sel = functools.partial(_dsa_select_kernel, pos0=pos0, s_real=s_real, ts=ts, top_k=top_k,
                            wi_scale=(IDX_HEADS * IDX_DIM) ** -0.5)
    bias = pl.pallas_call(
        sel,
        grid=(nb, t // tq_sel),
        in_specs=[pl.BlockSpec((None, tq_sel, qi.shape[2]), lambda b, i: (b, i, 0)),
                  pl.BlockSpec((None, tq_sel, wi.shape[2]), lambda b, i: (b, i, 0)),
                  pl.BlockSpec((None, IDX_DIM, s_pad), lambda b, i: (b, 0, 0))],
        out_specs=pl.BlockSpec((None, tq_sel, s_pad), lambda b, i: (b, i, 0)),
        out_shape=jax.ShapeDtypeStruct((nb, t, s_pad), F32),
        scratch_shapes=[pltpu.VMEM((tq_sel, LANES), F32),
                        pltpu.VMEM((IDX_HEADS, tq_sel, IDX_DIM), BF16),
                        pltpu.VMEM((IDX_HEADS, tq_sel, LANES), F32)],
        compiler_params=params,
        name="dsa_select",
    )(qi, wi, kit)
    att = functools.partial(_dsa_attend_kernel, pos0=pos0, s_real=s_real, ts=ts, rows=rows)
    tile3 = lambda w, dt: pltpu.VMEM((N_HEADS, tq_att, w), dt)
    return pl.pallas_call(
        att,
        grid=(nb, t // tq_att),
        in_specs=[pl.BlockSpec((None, tq_att, nq), lambda b, i: (b, i, 0)),
                  pl.BlockSpec((None, tq_att, s_pad), lambda b, i: (b, i, 0)),
                  pl.BlockSpec((None, N_KV_HEADS, HEAD_DIM, s_pad), lambda b, i: (b, 0, 0, 0),
                               pipeline_mode=pl.Buffered(1)),
                  pl.BlockSpec((None, N_KV_HEADS, s_pad, LANES), lambda b, i: (b, 0, 0, 0),
                               pipeline_mode=pl.Buffered(1))],
        out_specs=pl.BlockSpec((None, tq_att, nq), lambda b, i: (b, i, 0)),
        out_shape=jax.ShapeDtypeStruct((nb, t, nq), F32),
        scratch_shapes=[tile3(HEAD_DIM, BF16), tile3(LANES, F32), tile3(LANES, F32)],
        compiler_params=params,
        name="dsa_attend",
    )(q, bias, kt, v)


def _tok_b_kernel(x1_ref, yc_ref, ya_ref, mk_ref, mv_ref, wmo_ref, g2_ref, b2_ref,
                  wq_ref, wo_ref, g3_ref, b3_ref, wg_ref, wu_ref, wd_ref, g4_ref, b4_ref,
                  y_ref, acc_ref, *, alpha, f_chunk):
    x1 = x1_ref[...]
    mixed = jnp.concatenate([yc_ref[...], ya_ref[...]], axis=-1).astype(BF16)
    x2 = _layer_norm(alpha * x1 + _dot(mixed, wmo_ref[...]), g2_ref[...], b2_ref[...])

    d = x2.shape[1]
    dh = d // MEM_HEADS
    qm = (_dot(x2.astype(BF16), wq_ref[...]) * (dh ** -0.5)).astype(BF16)
    mk = mk_ref[...].astype(BF16)
    mv = mv_ref[...].astype(BF16)
    heads = []
    for h in range(MEM_HEADS):
        sl = slice(h * dh, (h + 1) * dh)
        s = _dot_nt(qm[:, sl], mk[:, sl])
        e = jnp.exp(s - jnp.max(s, axis=-1, keepdims=True))
        p = e / jnp.sum(e, axis=-1, keepdims=True)
        heads.append(_dot(p.astype(BF16), mv[:, sl]))
    o = jnp.concatenate(heads, axis=-1).astype(BF16)
    x3 = _layer_norm(alpha * x2 + _dot(o, wo_ref[...]), g3_ref[...], b3_ref[...])

    _swiglu_into(acc_ref, x3.astype(BF16), wg_ref, wu_ref, wd_ref, f_chunk)
    y_ref[...] = _layer_norm(alpha * x3 + 0.5 * acc_ref[...], g4_ref[...], b4_ref[...])


def _tok_b(x1, yc, ya, mk, mv, wmo, g2, b2, wq, wo, g3, b3, wg, wu, wd, g4, b4,
           *, alpha, tm, seg_len):
    n, d = x1.shape
    n_mem = mk.shape[1]
    tiles_per_seg = seg_len // tm
    row = lambda w: pl.BlockSpec((tm, w), lambda i: (i, 0))
    mem = pl.BlockSpec((None, n_mem, d), lambda i: (i // tiles_per_seg, 0, 0))
    consts = [wmo, g2, b2, wq, wo, g3, b3, wg, wu, wd, g4, b4]
    kern = functools.partial(_tok_b_kernel, alpha=alpha, f_chunk=FFN_CHUNK)
    return pl.pallas_call(
        kern,
        grid=(n // tm,),
        in_specs=[row(d), row(yc.shape[1]), row(ya.shape[1]), mem, mem]
                 + [_const_spec(c.shape) for c in consts],
        out_specs=row(d),
        out_shape=jax.ShapeDtypeStruct((n, d), F32),
        scratch_shapes=[pltpu.VMEM((tm, d), F32)],
        compiler_params=pltpu.CompilerParams(dimension_semantics=("arbitrary",),
                                             vmem_limit_bytes=VMEM_LIMIT),
        name="tok_b",
    )(x1, yc, ya, mk, mv, *consts)


def _round_up(x, m):
    return (x + m - 1) // m * m


def _layer(x, conv_prev, k_past, v_past, ik_past, mem_k, mem_v, p, *, alpha):
    b, t, d = x.shape
    past = k_past.shape[1]
    d_conv = p["conv_w"].shape[1]
    top_k = min(TOPK_MAX, (past + t) // 4)
    tm = min(TOKEN_TILE, t)
    ts = DSA_KEY_TILE

    cprev = jnp.pad(conv_prev, ((0, 0), (6, 0), (0, 0)))
    x1, yconv, q, k, v, qi, tail, ulast = _tok_a(
        x.reshape(b * t, d), cprev, p["ffn1_gate"], p["ffn1_up"], p["ffn1_down"],
        p["ln1_g"], p["ln1_b"], p["w_mix_in"], p["conv_w"], alpha=alpha, tm=tm, seg_len=t)
    new_conv = ulast.reshape(b, t // tm, 8, d_conv)[:, -1, 6:8, :]
    k = k.reshape(b, t, N_KV_HEADS, HEAD_DIM)
    v = v.reshape(b, t, N_KV_HEADS, HEAD_DIM)
    ki = tail[:, :IDX_DIM].reshape(b, t, IDX_DIM)
    wi = tail[:, IDX_DIM:IDX_DIM + IDX_HEADS].reshape(b, t, IDX_HEADS)

    s_real = past + t
    s_pad = _round_up(s_real, ts)
    pad = lambda a: jnp.pad(a, ((0, 0), (0, s_pad - s_real)) + ((0, 0),) * (a.ndim - 2))
    k_all = pad(jnp.concatenate([k_past, k], axis=1)).astype(BF16)
    v_all = pad(jnp.concatenate([v_past, v], axis=1)).astype(BF16)
    ki_all = pad(jnp.concatenate([ik_past, ki], axis=1)).astype(BF16)
    ones_col = (jnp.arange(LANES - HEAD_DIM) == 0).astype(BF16)
    v_aug = jnp.concatenate(
        [v_all.transpose(0, 2, 1, 3),
         jnp.broadcast_to(ones_col, (b, N_KV_HEADS, s_pad, LANES - HEAD_DIM))], axis=-1)
    y_attn = _dsa(q.reshape(b, t, -1), qi.reshape(b, t, -1), wi,
                  ki_all.transpose(0, 2, 1), k_all.transpose(0, 2, 3, 1), v_aug,
                  pos0=past, s_real=s_real, tq_sel=min(DSA_SELECT_ROWS, t),
                  tq_att=min(DSA_ATTEND_ROWS, t), rows=min(DSA_ATTEND_BLOCK, t), ts=ts, top_k=top_k)

    y = _tok_b(x1, yconv, y_attn.reshape(b * t, -1), mem_k.reshape(b, mem_k.shape[1], d),
               mem_v.reshape(b, mem_v.shape[1], d),
               p["w_mix_out"], p["ln2_g"], p["ln2_b"], p["w_mem_q"], p["w_mem_o"],
               p["ln3_g"], p["ln3_b"], p["ffn2_gate"], p["ffn2_up"], p["ffn2_down"],
               p["ln4_g"], p["ln4_b"], alpha=alpha, tm=tm, seg_len=t)
    return y.reshape(b, t, d), new_conv, k, v, ki


def kernel(x_prompt, x_sample, cache_conv, cache_k, cache_v, cache_idx_k, cache_mem_k, cache_mem_v,
           mem_prompt, ffn1_gate, ffn1_up, ffn1_down, ln1_g, ln1_b, w_mix_in, conv_w, w_mix_out,
           ln2_g, ln2_b, w_mem_q, w_mem_k, w_mem_v, w_mem_o, ln3_g, ln3_b,
           ffn2_gate, ffn2_up, ffn2_down, ln4_g, ln4_b):
    depth = ffn1_gate.shape[0]
    alpha = (2.0 * depth) ** 0.25
    b_p, t_p, d = x_prompt.shape
    dt = x_prompt.dtype
    d_conv = conv_w.shape[2]
    n_mem = mem_prompt.shape[1]
    bf = lambda w: w.astype(BF16)
    vec = lambda a: a.reshape(1, -1)

    y_p, y_s = x_prompt, x_sample
    outs_p = [[] for _ in range(6)]
    outs_s = [[] for _ in range(4)]
    for l in range(depth):
        d_in = w_mix_in.shape[2]
        p = dict(ffn1_gate=bf(ffn1_gate[l]), ffn1_up=bf(ffn1_up[l]), ffn1_down=bf(ffn1_down[l]),
                 ln1_g=vec(ln1_g[l]), ln1_b=vec(ln1_b[l]),
                 w_mix_in=jnp.pad(bf(w_mix_in[l]),
                                  ((0, 0), (0, _round_up(d_in, LANES) - d_in))),
                 conv_w=conv_w[l], w_mix_out=bf(w_mix_out[l]),
                 ln2_g=vec(ln2_g[l]), ln2_b=vec(ln2_b[l]),
                 w_mem_q=bf(w_mem_q[l]), w_mem_o=bf(w_mem_o[l]),
                 ln3_g=vec(ln3_g[l]), ln3_b=vec(ln3_b[l]),
                 ffn2_gate=bf(ffn2_gate[l]), ffn2_up=bf(ffn2_up[l]), ffn2_down=bf(ffn2_down[l]),
                 ln4_g=vec(ln4_g[l]), ln4_b=vec(ln4_b[l]))
        mem_kv = _memkv(mem_prompt.reshape(b_p * n_mem, d),
                        jnp.stack([bf(w_mem_k[l]), bf(w_mem_v[l])]))
        mem_k = mem_kv[0].reshape(b_p, n_mem, MEM_HEADS, d // MEM_HEADS)
        mem_v = mem_kv[1].reshape(b_p, n_mem, MEM_HEADS, d // MEM_HEADS)
        y_p, c_new, k_new, v_new, ik_new = _layer(
            y_p, jnp.zeros((b_p, 2, d_conv), dt),
            jnp.zeros((b_p, 0, N_KV_HEADS, HEAD_DIM), dt), jnp.zeros((b_p, 0, N_KV_HEADS, HEAD_DIM), dt),
            jnp.zeros((b_p, 0, IDX_DIM), dt), mem_k, mem_v, p, alpha=alpha)
        for lst, a in zip(outs_p, (c_new, k_new, v_new, ik_new, mem_k, mem_v)):
            lst.append(a)
        y_s, c_new, k_new, v_new, ik_new = _layer(
            y_s, cache_conv[l], cache_k[l], cache_v[l], cache_idx_k[l],
            cache_mem_k[l], cache_mem_v[l], p, alpha=alpha)
        for lst, a in zip(outs_s, (c_new, k_new, v_new, ik_new)):
            lst.append(a)
    return (y_p, y_s, *[jnp.stack(o) for o in outs_p], *[jnp.stack(o) for o in outs_s])
```

```python
import functools

import jax
import jax.numpy as jnp
from jax import lax
from jax.experimental import pallas as pl
from jax.experimental.pallas import tpu as pltpu

F32 = jnp.float32
BF16 = jnp.bfloat16
I32 = jnp.int32

CHUNK = 64
N_HEADS = 8
HEAD_DIM = 64
N_KV_HEADS = 2
GROUP = N_HEADS // N_KV_HEADS
IDX_HEADS = 8
IDX_DIM = 64
TOPK_MAX = 256
MEM_HEADS = 4
LN_EPS = 1e-5

LANES = 128
VMEM_LIMIT = 56 * 1024 * 1024
NEG_BIG = -1e30
F32_LOWEST = -3.4028234663852886e38
KEY_NEG_INF = -2139095041

TOKEN_TILE = 512
DSA_SELECT_ROWS = 128
DSA_ATTEND_ROWS = 512
DSA_ATTEND_BLOCK = 128
DSA_KEY_TILE = 256
FFN_CHUNK = 256


def _dot(a, b):
    return jnp.dot(a, b, preferred_element_type=F32)


def _dot_nt(a, b):
    return lax.dot_general(a, b, (((1,), (1,)), ((), ())), preferred_element_type=F32)


def _layer_norm(z, g, b):
    mu = jnp.mean(z, axis=-1, keepdims=True)
    d = z - mu
    var = jnp.mean(d * d, axis=-1, keepdims=True)
    return d * lax.rsqrt(var + LN_EPS) * g + b


def _swiglu_into(acc_ref, xb, wg_ref, wu_ref, wd_ref, f_chunk):
    d_ff = wg_ref.shape[1]
    for c in range(d_ff // f_chunk):
        sl = slice(c * f_chunk, (c + 1) * f_chunk)
        g = _dot(xb, wg_ref[:, sl])
        u = _dot(xb, wu_ref[:, sl])
        h = (g * (1.0 / (1.0 + jnp.exp(-g)))) * u
        part = _dot(h.astype(BF16), wd_ref[sl, :])
        if c == 0:
            acc_ref[...] = part
        else:
            acc_ref[...] += part


def _tok_a_kernel(x_ref, cprev_ref, wg_ref, wu_ref, wd_ref, g_ref, b_ref, win_ref, cw_ref,
                  x1_ref, yconv_ref, q_ref, k_ref, v_ref, qi_ref, tail_ref, ulast_ref,
                  acc_ref, ubuf_ref, *, alpha, tiles_per_seg, f_chunk, d_conv):
    i = pl.program_id(0)
    tm = x_ref.shape[0]
    x = x_ref[...]
    _swiglu_into(acc_ref, x.astype(BF16), wg_ref, wu_ref, wd_ref, f_chunk)
    x1 = _layer_norm(alpha * x + 0.5 * acc_ref[...], g_ref[...], b_ref[...])
    x1_ref[...] = x1
    x1b = x1.astype(BF16)

    c0 = 0
    h = _dot(x1b, win_ref[:, c0:c0 + d_conv]); c0 += d_conv
    gate_b = _dot(x1b, win_ref[:, c0:c0 + d_conv]); c0 += d_conv
    gate_c = _dot(x1b, win_ref[:, c0:c0 + d_conv]); c0 += d_conv
    nq = N_HEADS * HEAD_DIM
    q_ref[...] = (_dot(x1b, win_ref[:, c0:c0 + nq]) * (HEAD_DIM ** -0.5)).astype(BF16); c0 += nq
    nkv = N_KV_HEADS * HEAD_DIM
    k_ref[...] = _dot(x1b, win_ref[:, c0:c0 + nkv]); c0 += nkv
    v_ref[...] = _dot(x1b, win_ref[:, c0:c0 + nkv]); c0 += nkv
    nqi = IDX_HEADS * IDX_DIM
    qi_ref[...] = _dot(x1b, win_ref[:, c0:c0 + nqi]).astype(BF16); c0 += nqi
    tail_ref[...] = _dot(x1b, win_ref[:, c0:c0 + LANES])

    u = gate_c * h

    @pl.when(i % tiles_per_seg == 0)
    def _():
        ubuf_ref[0:8, :] = cprev_ref[0]

    ubuf_ref[8:tm + 8, :] = u
    um1 = ubuf_ref[7:tm + 7, :]
    um2 = ubuf_ref[6:tm + 6, :]
    cw = cw_ref[...]
    conv = cw[0:1, :] * um2 + cw[1:2, :] * um1 + cw[2:3, :] * u
    yconv_ref[...] = gate_b * conv
    last8 = ubuf_ref[tm:tm + 8, :]
    ulast_ref[0] = last8
    ubuf_ref[0:8, :] = last8


def _const_spec(shape):
    nd = len(shape)
    return pl.BlockSpec(shape, lambda *_: (0,) * nd, pipeline_mode=pl.Buffered(1))


def _tok_a(x, cprev, wg, wu, wd, g, b, win, cw, *, alpha, tm, seg_len):
    n, d = x.shape
    d_conv = cw.shape[1]
    n_tiles = n // tm
    row = lambda w: pl.BlockSpec((tm, w), lambda i: (i, 0))
    nq, nkv, nqi = N_HEADS * HEAD_DIM, N_KV_HEADS * HEAD_DIM, IDX_HEADS * IDX_DIM
    tiles_per_seg = seg_len // tm
    kern = functools.partial(_tok_a_kernel, alpha=alpha, tiles_per_seg=tiles_per_seg,
                             f_chunk=FFN_CHUNK, d_conv=d_conv)
    return pl.pallas_call(
        kern,
        grid=(n_tiles,),
        in_specs=[row(d),
                  pl.BlockSpec((1, 8, d_conv), lambda i: (i // tiles_per_seg, 0, 0)),
                  _const_spec(wg.shape), _const_spec(wu.shape), _const_spec(wd.shape),
                  _const_spec(g.shape), _const_spec(b.shape), _const_spec(win.shape),
                  _const_spec(cw.shape)],
        out_specs=[row(d), row(d_conv), row(nq), row(nkv), row(nkv), row(nqi), row(LANES),
                   pl.BlockSpec((1, 8, d_conv), lambda i: (i, 0, 0))],
        out_shape=[jax.ShapeDtypeStruct((n, d), F32),
                   jax.ShapeDtypeStruct((n, d_conv), F32),
                   jax.ShapeDtypeStruct((n, nq), BF16),
                   jax.ShapeDtypeStruct((n, nkv), F32),
                   jax.ShapeDtypeStruct((n, nkv), F32),
                   jax.ShapeDtypeStruct((n, nqi), BF16),
                   jax.ShapeDtypeStruct((n, LANES), F32),
                   jax.ShapeDtypeStruct((n_tiles, 8, d_conv), F32)],
        scratch_shapes=[pltpu.VMEM((tm, d), F32), pltpu.VMEM((tm + 8, d_conv), F32)],
        compiler_params=pltpu.CompilerParams(dimension_semantics=("arbitrary",),
                                             vmem_limit_bytes=VMEM_LIMIT),
        name="tok_a",
    )(x, cprev, wg, wu, wd, g, b, win, cw)


def _memkv_kernel(m_ref, w_ref, o_ref):
    o_ref[0] = _dot(m_ref[...].astype(BF16), w_ref[0])


def _memkv(mem, w2):
    n, d = mem.shape
    dout = w2.shape[2]
    return pl.pallas_call(
        _memkv_kernel,
        grid=(2,),
        in_specs=[pl.BlockSpec((n, d), lambda j: (0, 0)),
                  pl.BlockSpec((1, d, dout), lambda j: (j, 0, 0))],
        out_specs=pl.BlockSpec((1, n, dout), lambda j: (j, 0, 0)),
        out_shape=jax.ShapeDtypeStruct((2, n, dout), F32),
        compiler_params=pltpu.CompilerParams(dimension_semantics=("arbitrary",),
                                             vmem_limit_bytes=VMEM_LIMIT),
        name="memkv",
    )(mem, w2)


def _float_to_key(x):
    b = lax.bitcast_convert_type(x, I32)
    return jnp.where(b >= 0, b, b ^ 0x7FFFFFFF)


def _key_to_float(key):
    bits = jnp.where(key >= 0, key, key ^ 0x7FFFFFFF)
    return lax.bitcast_convert_type(bits, F32)


def _lane_fold(m, op):
    part = m[:, 0:LANES]
    for c in range(1, m.shape[1] // LANES):
        part = op(part, m[:, c * LANES:(c + 1) * LANES])
    return part


def _admissible_keys(pos0, q0, tq, s_real):
    return jnp.minimum(((pos0 + q0 + tq - 1) // CHUNK + 1) * CHUNK, s_real)


def _dsa_score_kernel(qi_ref, wi_ref, kit_ref, sc_ref, rmax_ref, qis_ref, wb_ref,
                      *, pos0, s_real, ts, rows, wi_scale):
    tq = qi_ref.shape[0]
    row_blocks = [slice(r * rows, (r + 1) * rows) for r in range(tq // rows)]
    q0 = pl.program_id(1) * tq
    nkt = (_admissible_keys(pos0, q0, tq, s_real) + ts - 1) // ts

    def lane_tile(a):
        return jnp.concatenate([a] * (ts // LANES), axis=1)

    wi = wi_ref[...] * wi_scale
    for h in range(IDX_HEADS):
        qis_ref[h] = qi_ref[:, h * IDX_DIM:(h + 1) * IDX_DIM]
        wb_ref[h] = jnp.broadcast_to(wi[:, h:h + 1], (tq, LANES))
    q_chunk = lax.shift_right_arithmetic(
        pos0 + q0 + lax.broadcasted_iota(I32, (tq, 1), 0), 6)
    rmax_ref[...] = jnp.full(rmax_ref.shape, -jnp.inf, F32)

    def score_body(j, carry):
        off = pl.multiple_of(j * ts, ts)
        kit = kit_ref[:, pl.ds(off, ts)]
        kidx = j * ts + lax.broadcasted_iota(I32, (1, ts), 1)
        k_chunk = lax.shift_right_arithmetic(kidx, 6)
        for rs in row_blocks:
            acc = None
            for h in range(IDX_HEADS):
                term = lane_tile(wb_ref[h, rs, :]) * jnp.maximum(_dot(qis_ref[h, rs, :], kit), 0.0)
                acc = term if acc is None else acc + term
            admissible = (k_chunk <= q_chunk[rs]) & (kidx < s_real)
            s = jnp.where(admissible, acc, -jnp.inf)
            sc_ref[rs, pl.ds(off, ts)] = s
            rmax_ref[rs, :] = jnp.maximum(rmax_ref[rs, :], _lane_fold(s, jnp.maximum))
        return carry

    lax.fori_loop(0, nkt, score_body, 0)

    def fill_body(j, carry):
        sc_ref[:, pl.ds(pl.multiple_of(j * ts, ts), ts)] = jnp.full((tq, ts), -jnp.inf, F32)
        return carry

    lax.fori_loop(nkt, sc_ref.shape[1] // ts, fill_body, 0)


def _dsa_select_kernel(sc_ref, rmax_ref, bias_ref, *, pos0, s_real, ts, top_k):
    tq = sc_ref.shape[0]
    q0 = pl.program_id(1) * tq
    nkt = (_admissible_keys(pos0, q0, tq, s_real) + ts - 1) // ts
    k_f = float(top_k)

    def tile_off(j):
        return pl.multiple_of(j * ts, ts)

    def key_index(j):
        return j * ts + lax.broadcasted_iota(I32, (1, ts), 1)

    def count_rows(pred):
        def body(j, a):
            s = sc_ref[:, pl.ds(tile_off(j), ts)]
            return a + _lane_fold(jnp.where(pred(s, j), 1.0, 0.0), jnp.add)
        a = lax.fori_loop(0, nkt, body, jnp.zeros((tq, LANES), F32))
        return jnp.sum(a, axis=1, keepdims=True)

    def any_row(flag):
        return jnp.max(jnp.where(flag, 1.0, 0.0)) > 0.5

    def key_mid(lo, hi):
        return (lax.shift_right_arithmetic(lo, 1) + lax.shift_right_arithmetic(hi, 1)
                + (lo & hi & 1))

    def probe(lo, hi, cnt_lo, mid):
        thr = _key_to_float(mid)
        cnt = count_rows(lambda s, j: s >= thr)
        ge = cnt >= k_f
        exact = cnt == k_f
        lo_n = jnp.where(ge, mid, lo)
        hi_n = jnp.where(exact, mid + 1, jnp.where(ge, hi, mid))
        return lo_n, hi_n, jnp.where(ge, cnt, cnt_lo)

    def two_probes(c):
        c = probe(*c, key_mid(c[0], c[1]))
        return probe(*c, key_mid(c[0], c[1]))

    row_max = jnp.max(rmax_ref[...], axis=1, keepdims=True)
    lo0 = jnp.full((tq, 1), KEY_NEG_INF, I32)
    hi0 = _float_to_key(row_max) + 1
    cnt0 = jnp.zeros((tq, 1), F32) + (nkt * ts).astype(F32)
    mid0 = jnp.where(row_max > 0.0, _float_to_key(row_max * 0.25), key_mid(lo0, hi0))
    lo, _, cnt_lo = lax.while_loop(lambda c: any_row(c[0] + 1 < c[1]), two_probes,
                                   probe(lo0, hi0, cnt0, mid0))
    thr = _key_to_float(lo)
    thr_fin = jnp.maximum(thr, F32_LOWEST)
    excess = jnp.where(thr == -jnp.inf, 0.0, cnt_lo - k_f)
    tied = any_row(excess > 0.5)

    @pl.when(tied)
    def _():
        def tie_cond(c):
            jl, jh = c
            return any_row(jl + 1 < jh)

        def tie_body(c):
            jl, jh = c
            mid = lax.shift_right_arithmetic(jl + jh, 1)
            cnt = count_rows(lambda s, j: (s == thr) & (key_index(j) >= mid))
            ge = cnt >= excess
            return jnp.where(ge, mid, jl), jnp.where(ge, jh, mid)

        jl, _ = lax.while_loop(tie_cond, tie_body,
                               (jnp.zeros((tq, 1), I32), jnp.zeros((tq, 1), I32) + nkt * ts))

        def bias_body(j, carry):
            off = tile_off(j)
            s = sc_ref[:, pl.ds(off, ts)]
            drop = (s == thr) & (key_index(j) >= jl) & (excess > 0.5)
            bias_ref[:, pl.ds(off, ts)] = jnp.where((s >= thr_fin) & ~drop, 0.0, NEG_BIG)
            return carry

        lax.fori_loop(0, nkt, bias_body, 0)

    @pl.when(jnp.logical_not(tied))
    def _():
        def bias_body(j, carry):
            off = tile_off(j)
            s = sc_ref[:, pl.ds(off, ts)]
            bias_ref[:, pl.ds(off, ts)] = jnp.where(s >= thr_fin, 0.0, NEG_BIG)
            return carry

        lax.fori_loop(0, nkt, bias_body, 0)

    def fill_body(j, carry):
        bias_ref[:, pl.ds(tile_off(j), ts)] = jnp.full((tq, ts), NEG_BIG, F32)
        return carry

    lax.fori_loop(nkt, bias_ref.shape[1] // ts, fill_body, 0)


def _dsa_attend_kernel(q_ref, bias_ref, kt_ref, v_ref, o_ref, qs_ref, m_ref, acc_ref,
                       *, pos0, s_real, ts, rows, hs):
    tq = q_ref.shape[0]
    row_blocks = [slice(r * rows, (r + 1) * rows) for r in range(tq // rows)]
    n_sets = N_HEADS // hs
    q0 = pl.program_id(1) * tq
    nkt = (_admissible_keys(pos0, q0, tq, s_real) + ts - 1) // ts

    def lane_tile(a):
        return jnp.concatenate([a] * (ts // LANES), axis=1)

    def stacked(rs):
        return slice(rs.start * hs, rs.stop * hs)

    for st in range(n_sets):
        for rs in row_blocks:
            qs_ref[st, stacked(rs), :] = jnp.concatenate(
                [q_ref[rs, h * HEAD_DIM:(h + 1) * HEAD_DIM] for h in range(st * hs, (st + 1) * hs)],
                axis=0)
    m_ref[...] = jnp.full(m_ref.shape, NEG_BIG, F32)
    acc_ref[...] = jnp.zeros(acc_ref.shape, F32)

    def att_body(j, carry):
        off = pl.multiple_of(j * ts, ts)
        for rs in row_blocks:
            bias = jnp.concatenate([bias_ref[rs, pl.ds(off, ts)]] * hs, axis=0)
            srs = stacked(rs)
            for st in range(n_sets):
                g = (st * hs) // GROUP
                s = _dot(qs_ref[st, srs, :], kt_ref[g, :, pl.ds(off, ts)]) + bias
                m_old = m_ref[st, srs, :]
                m_new = jnp.maximum(m_old, jnp.max(s, axis=1, keepdims=True))
                p = jnp.exp(s - lane_tile(m_new)).astype(BF16)
                acc_ref[st, srs, :] = (jnp.exp(m_old - m_new) * acc_ref[st, srs, :]
                                       + _dot(p, v_ref[g, pl.ds(off, ts), :]))
                m_ref[st, srs, :] = m_new
        return carry

    lax.fori_loop(0, nkt, att_body, 0)
    for st in range(n_sets):
        for rs in row_blocks:
            for i in range(hs):
                h = st * hs + i
                part = slice(rs.start * hs + i * rows, rs.start * hs + (i + 1) * rows)
                acc = acc_ref[st, part, :]
                o_ref[rs, h * HEAD_DIM:(h + 1) * HEAD_DIM] = (
                    acc[:, 0:HEAD_DIM] / acc[:, HEAD_DIM:HEAD_DIM + 1])


def _dsa(q, qi, wi, kit, kt, v, *, pos0, s_real, tq_sel, tq_att, rows, ts, top_k):
    nb, t, nq = q.shape
    s_pad = kit.shape[2]
    assert s_pad % ts == 0 and t % tq_sel == 0 and t % tq_att == 0 and tq_att % rows == 0
    assert top_k <= ts and top_k <= s_real
    params = pltpu.CompilerParams(dimension_semantics=("arbitrary", "arbitrary"),
                                  vmem_limit_bytes=VMEM_LIMIT)
    score = functools.partial(_dsa_score_kernel, pos0=pos0, s_real=s_real, ts=ts, rows=rows,
                              wi_scale=(IDX_HEADS * IDX_DIM) ** -0.5)
    scores, row_max = pl.pallas_call(
        score,
        grid=(nb, t // tq_att),
        in_specs=[pl.BlockSpec((None, tq_att, qi.shape[2]), lambda b, i: (b, i, 0)),
                  pl.BlockSpec((None, tq_att, wi.shape[2]), lambda b, i: (b, i, 0)),
                  pl.BlockSpec((None, IDX_DIM, s_pad), lambda b, i: (b, 0, 0),
                               pipeline_mode=pl.Buffered(1))],
        out_specs=[pl.BlockSpec((None, tq_att, s_pad), lambda b, i: (b, i, 0)),
                   pl.BlockSpec((None, tq_att, LANES), lambda b, i: (b, i, 0))],
        out_shape=[jax.ShapeDtypeStruct((nb, t, s_pad), F32),
                   jax.ShapeDtypeStruct((nb, t, LANES), F32)],
        scratch_shapes=[pltpu.VMEM((IDX_HEADS, tq_att, IDX_DIM), BF16),
                        pltpu.VMEM((IDX_HEADS, tq_att, LANES), F32)],
        compiler_params=params,
        name="dsa_score",
    )(qi, wi, kit)
    sel = functools.partial(_dsa_select_kernel, pos0=pos0, s_real=s_real, ts=ts, top_k=top_k)
    bias = pl.pallas_call(
        sel,
        grid=(nb, t // tq_sel),
        in_specs=[pl.BlockSpec((None, tq_sel, s_pad), lambda b, i: (b, i, 0)),
                  pl.BlockSpec((None, tq_sel, LANES), lambda b, i: (b, i, 0))],
        out_specs=pl.BlockSpec((None, tq_sel, s_pad), lambda b, i: (b, i, 0)),
        out_shape=jax.ShapeDtypeStruct((nb, t, s_pad), F32),
        compiler_params=params,
        name="dsa_select",
    )(scores, row_max)
    hs = max(1, min(GROUP, DSA_ATTEND_BLOCK // rows))
    att = functools.partial(_dsa_attend_kernel, pos0=pos0, s_real=s_real, ts=ts, rows=rows, hs=hs)
    tile3 = lambda w, dt: pltpu.VMEM((N_HEADS // hs, hs * tq_att, w), dt)
    return pl.pallas_call(
        att,
        grid=(nb, t // tq_att),
        in_specs=[pl.BlockSpec((None, tq_att, nq), lambda b, i: (b, i, 0)),
                  pl.BlockSpec((None, tq_att, s_pad), lambda b, i: (b, i, 0)),
                  pl.BlockSpec((None, N_KV_HEADS, HEAD_DIM, s_pad), lambda b, i: (b, 0, 0, 0),
                               pipeline_mode=pl.Buffered(1)),
                  pl.BlockSpec((None, N_KV_HEADS, s_pad, LANES), lambda b, i: (b, 0, 0, 0),
                               pipeline_mode=pl.Buffered(1))],
        out_specs=pl.BlockSpec((None, tq_att, nq), lambda b, i: (b, i, 0)),
        out_shape=jax.ShapeDtypeStruct((nb, t, nq), F32),
        scratch_shapes=[tile3(HEAD_DIM, BF16), tile3(LANES, F32), tile3(LANES, F32)],
        compiler_params=params,
        name="dsa_attend",
    )(q, bias, kt, v)


def _tok_b_kernel(x1_ref, yc_ref, ya_ref, mk_ref, mv_ref, wmo_ref, g2_ref, b2_ref,
                  wq_ref, wo_ref, g3_ref, b3_ref, wg_ref, wu_ref, wd_ref, g4_ref, b4_ref,
                  y_ref, acc_ref, *, alpha, f_chunk):
    x1 = x1_ref[...]
    mixed = jnp.concatenate([yc_ref[...], ya_ref[...]], axis=-1).astype(BF16)
    x2 = _layer_norm(alpha * x1 + _dot(mixed, wmo_ref[...]), g2_ref[...], b2_ref[...])

    d = x2.shape[1]
    dh = d // MEM_HEADS
    qm = (_dot(x2.astype(BF16), wq_ref[...]) * (dh ** -0.5)).astype(BF16)
    mk = mk_ref[...].astype(BF16)
    mv = mv_ref[...].astype(BF16)
    heads = []
    for h in range(MEM_HEADS):
        sl = slice(h * dh, (h + 1) * dh)
        s = _dot_nt(qm[:, sl], mk[:, sl])
        e = jnp.exp(s - jnp.max(s, axis=-1, keepdims=True))
        p = e / jnp.sum(e, axis=-1, keepdims=True)
        heads.append(_dot(p.astype(BF16), mv[:, sl]))
    o = jnp.concatenate(heads, axis=-1).astype(BF16)
    x3 = _layer_norm(alpha * x2 + _dot(o, wo_ref[...]), g3_ref[...], b3_ref[...])

    _swiglu_into(acc_ref, x3.astype(BF16), wg_ref, wu_ref, wd_ref, f_chunk)
    y_ref[...] = _layer_norm(alpha * x3 + 0.5 * acc_ref[...], g4_ref[...], b4_ref[...])


def _tok_b(x1, yc, ya, mk, mv, wmo, g2, b2, wq, wo, g3, b3, wg, wu, wd, g4, b4,
           *, alpha, tm, seg_len):
    n, d = x1.shape
    n_mem = mk.shape[1]
    tiles_per_seg = seg_len // tm
    row = lambda w: pl.BlockSpec((tm, w), lambda i: (i, 0))
    mem = pl.BlockSpec((None, n_mem, d), lambda i: (i // tiles_per_seg, 0, 0))
    consts = [wmo, g2, b2, wq, wo, g3, b3, wg, wu, wd, g4, b4]
    kern = functools.partial(_tok_b_kernel, alpha=alpha, f_chunk=FFN_CHUNK)
    return pl.pallas_call(
        kern,
        grid=(n // tm,),
        in_specs=[row(d), row(yc.shape[1]), row(ya.shape[1]), mem, mem]
                 + [_const_spec(c.shape) for c in consts],
        out_specs=row(d),
        out_shape=jax.ShapeDtypeStruct((n, d), F32),
        scratch_shapes=[pltpu.VMEM((tm, d), F32)],
        compiler_params=pltpu.CompilerParams(dimension_semantics=("arbitrary",),
                                             vmem_limit_bytes=VMEM_LIMIT),
        name="tok_b",
    )(x1, yc, ya, mk, mv, *consts)


def _round_up(x, m):
    return (x + m - 1) // m * m


def _layer(x, conv_prev, k_past, v_past, ik_past, mem_k, mem_v, p, *, alpha):
    b, t, d = x.shape
    past = k_past.shape[1]
    d_conv = p["conv_w"].shape[1]
    top_k = min(TOPK_MAX, (past + t) // 4)
    tm = min(TOKEN_TILE, t)
    ts = DSA_KEY_TILE

    cprev = jnp.pad(conv_prev, ((0, 0), (6, 0), (0, 0)))
    x1, yconv, q, k, v, qi, tail, ulast = _tok_a(
        x.reshape(b * t, d), cprev, p["ffn1_gate"], p["ffn1_up"], p["ffn1_down"],
        p["ln1_g"], p["ln1_b"], p["w_mix_in"], p["conv_w"], alpha=alpha, tm=tm, seg_len=t)
    new_conv = ulast.reshape(b, t // tm, 8, d_conv)[:, -1, 6:8, :]
    k = k.reshape(b, t, N_KV_HEADS, HEAD_DIM)
    v = v.reshape(b, t, N_KV_HEADS, HEAD_DIM)
    ki = tail[:, :IDX_DIM].reshape(b, t, IDX_DIM)
    wi = tail[:, IDX_DIM:IDX_DIM + IDX_HEADS].reshape(b, t, IDX_HEADS)

    s_real = past + t
    s_pad = _round_up(s_real, ts)
    pad = lambda a: jnp.pad(a, ((0, 0), (0, s_pad - s_real)) + ((0, 0),) * (a.ndim - 2))
    k_all = pad(jnp.concatenate([k_past, k], axis=1)).astype(BF16)
    v_all = pad(jnp.concatenate([v_past, v], axis=1)).astype(BF16)
    ki_all = pad(jnp.concatenate([ik_past, ki], axis=1)).astype(BF16)
    ones_col = (jnp.arange(LANES - HEAD_DIM) == 0).astype(BF16)
    v_aug = jnp.concatenate(
        [v_all.transpose(0, 2, 1, 3),
         jnp.broadcast_to(ones_col, (b, N_KV_HEADS, s_pad, LANES - HEAD_DIM))], axis=-1)
    y_attn = _dsa(q.reshape(b, t, -1), qi.reshape(b, t, -1), wi,
                  ki_all.transpose(0, 2, 1), k_all.transpose(0, 2, 3, 1), v_aug,
                  pos0=past, s_real=s_real, tq_sel=min(DSA_SELECT_ROWS, t),
                  tq_att=min(DSA_ATTEND_ROWS, t), rows=min(DSA_ATTEND_BLOCK, t), ts=ts, top_k=top_k)

    y = _tok_b(x1, yconv, y_attn.reshape(b * t, -1), mem_k.reshape(b, mem_k.shape[1], d),
               mem_v.reshape(b, mem_v.shape[1], d),
               p["w_mix_out"], p["ln2_g"], p["ln2_b"], p["w_mem_q"], p["w_mem_o"],
               p["ln3_g"], p["ln3_b"], p["ffn2_gate"], p["ffn2_up"], p["ffn2_down"],
               p["ln4_g"], p["ln4_b"], alpha=alpha, tm=tm, seg_len=t)
    return y.reshape(b, t, d), new_conv, k, v, ki


def kernel(x_prompt, x_sample, cache_conv, cache_k, cache_v, cache_idx_k, cache_mem_k, cache_mem_v,
           mem_prompt, ffn1_gate, ffn1_up, ffn1_down, ln1_g, ln1_b, w_mix_in, conv_w, w_mix_out,
           ln2_g, ln2_b, w_mem_q, w_mem_k, w_mem_v, w_mem_o, ln3_g, ln3_b,
           ffn2_gate, ffn2_up, ffn2_down, ln4_g, ln4_b):
    depth = ffn1_gate.shape[0]
    alpha = (2.0 * depth) ** 0.25
    b_p, t_p, d = x_prompt.shape
    dt = x_prompt.dtype
    d_conv = conv_w.shape[2]
    n_mem = mem_prompt.shape[1]
    bf = lambda w: w.astype(BF16)
    vec = lambda a: a.reshape(1, -1)

    y_p, y_s = x_prompt, x_sample
    outs_p = [[] for _ in range(6)]
    outs_s = [[] for _ in range(4)]
    for l in range(depth):
        d_in = w_mix_in.shape[2]
        p = dict(ffn1_gate=bf(ffn1_gate[l]), ffn1_up=bf(ffn1_up[l]), ffn1_down=bf(ffn1_down[l]),
                 ln1_g=vec(ln1_g[l]), ln1_b=vec(ln1_b[l]),
                 w_mix_in=jnp.pad(bf(w_mix_in[l]),
                                  ((0, 0), (0, _round_up(d_in, LANES) - d_in))),
                 conv_w=conv_w[l], w_mix_out=bf(w_mix_out[l]),
                 ln2_g=vec(ln2_g[l]), ln2_b=vec(ln2_b[l]),
                 w_mem_q=bf(w_mem_q[l]), w_mem_o=bf(w_mem_o[l]),
                 ln3_g=vec(ln3_g[l]), ln3_b=vec(ln3_b[l]),
                 ffn2_gate=bf(ffn2_gate[l]), ffn2_up=bf(ffn2_up[l]), ffn2_down=bf(ffn2_down[l]),
                 ln4_g=vec(ln4_g[l]), ln4_b=vec(ln4_b[l]))
        mem_kv = _memkv(mem_prompt.reshape(b_p * n_mem, d),
                        jnp.stack([bf(w_mem_k[l]), bf(w_mem_v[l])]))
        mem_k = mem_kv[0].reshape(b_p, n_mem, MEM_HEADS, d // MEM_HEADS)
        mem_v = mem_kv[1].reshape(b_p, n_mem, MEM_HEADS, d // MEM_HEADS)
        y_p, c_new, k_new, v_new, ik_new = _layer(
            y_p, jnp.zeros((b_p, 2, d_conv), dt),
            jnp.zeros((b_p, 0, N_KV_HEADS, HEAD_DIM), dt), jnp.zeros((b_p, 0, N_KV_HEADS, HEAD_DIM), dt),
            jnp.zeros((b_p, 0, IDX_DIM), dt), mem_k, mem_v, p, alpha=alpha)
        for lst, a in zip(outs_p, (c_new, k_new, v_new, ik_new, mem_k, mem_v)):
            lst.append(a)
        y_s, c_new, k_new, v_new, ik_new = _layer(
            y_s, cache_conv[l], cache_k[l], cache_v[l], cache_idx_k[l],
            cache_mem_k[l], cache_mem_v[l], p, alpha=alpha)
        for lst, a in zip(outs_s, (c_new, k_new, v_new, ik_new)):
            lst.append(a)
    return (y_p, y_s, *[jnp.stack(o) for o in outs_p], *[jnp.stack(o) for o in outs_s])
```

```python
import functools

import jax
import jax.numpy as jnp
from jax import lax
from jax.experimental import pallas as pl
from jax.experimental.pallas import tpu as pltpu

F32 = jnp.float32
BF16 = jnp.bfloat16
I32 = jnp.int32

CHUNK = 64
N_HEADS = 8
HEAD_DIM = 64
N_KV_HEADS = 2
GROUP = N_HEADS // N_KV_HEADS
IDX_HEADS = 8
IDX_DIM = 64
TOPK_MAX = 256
MEM_HEADS = 4
LN_EPS = 1e-5

LANES = 128
VMEM_LIMIT = 56 * 1024 * 1024
NEG_BIG = -1e30
F32_LOWEST = -3.4028234663852886e38
KEY_NEG_INF = -2139095041

TOKEN_TILE = 512
DSA_SELECT_ROWS = 128
DSA_ATTEND_ROWS = 512
DSA_ATTEND_BLOCK = 128
DSA_KEY_TILE = 256
FFN_CHUNK = 256


def _dot(a, b):
    return jnp.dot(a, b, preferred_element_type=F32)


def _dot_nt(a, b):
    return lax.dot_general(a, b, (((1,), (1,)), ((), ())), preferred_element_type=F32)


def _layer_norm(z, g, b):
    mu = jnp.mean(z, axis=-1, keepdims=True)
    d = z - mu
    var = jnp.mean(d * d, axis=-1, keepdims=True)
    return d * lax.rsqrt(var + LN_EPS) * g + b


def _swiglu_into(acc_ref, xb, wg_ref, wu_ref, wd_ref, f_chunk):
    d_ff = wg_ref.shape[1]
    for c in range(d_ff // f_chunk):
        sl = slice(c * f_chunk, (c + 1) * f_chunk)
        g = _dot(xb, wg_ref[:, sl])
        u = _dot(xb, wu_ref[:, sl])
        h = (g * (1.0 / (1.0 + jnp.exp(-g)))) * u
        part = _dot(h.astype(BF16), wd_ref[sl, :])
        if c == 0:
            acc_ref[...] = part
        else:
            acc_ref[...] += part


def _tok_a_kernel(x_ref, cprev_ref, wg_ref, wu_ref, wd_ref, g_ref, b_ref, win_ref, cw_ref,
                  x1_ref, yconv_ref, q_ref, k_ref, v_ref, qi_ref, tail_ref, ulast_ref,
                  acc_ref, ubuf_ref, *, alpha, tiles_per_seg, f_chunk, d_conv):
    i = pl.program_id(0)
    tm = x_ref.shape[0]
    x = x_ref[...]
    _swiglu_into(acc_ref, x.astype(BF16), wg_ref, wu_ref, wd_ref, f_chunk)
    x1 = _layer_norm(alpha * x + 0.5 * acc_ref[...], g_ref[...], b_ref[...])
    x1_ref[...] = x1
    x1b = x1.astype(BF16)

    c0 = 0
    h = _dot(x1b, win_ref[:, c0:c0 + d_conv]); c0 += d_conv
    gate_b = _dot(x1b, win_ref[:, c0:c0 + d_conv]); c0 += d_conv
    gate_c = _dot(x1b, win_ref[:, c0:c0 + d_conv]); c0 += d_conv
    nq = N_HEADS * HEAD_DIM
    q_ref[...] = (_dot(x1b, win_ref[:, c0:c0 + nq]) * (HEAD_DIM ** -0.5)).astype(BF16); c0 += nq
    nkv = N_KV_HEADS * HEAD_DIM
    k_ref[...] = _dot(x1b, win_ref[:, c0:c0 + nkv]); c0 += nkv
    v_ref[...] = _dot(x1b, win_ref[:, c0:c0 + nkv]); c0 += nkv
    nqi = IDX_HEADS * IDX_DIM
    qi_ref[...] = _dot(x1b, win_ref[:, c0:c0 + nqi]).astype(BF16); c0 += nqi
    tail_ref[...] = _dot(x1b, win_ref[:, c0:c0 + LANES])

    u = gate_c * h

    @pl.when(i % tiles_per_seg == 0)
    def _():
        ubuf_ref[0:8, :] = cprev_ref[0]

    ubuf_ref[8:tm + 8, :] = u
    um1 = ubuf_ref[7:tm + 7, :]
    um2 = ubuf_ref[6:tm + 6, :]
    cw = cw_ref[...]
    conv = cw[0:1, :] * um2 + cw[1:2, :] * um1 + cw[2:3, :] * u
    yconv_ref[...] = gate_b * conv
    last8 = ubuf_ref[tm:tm + 8, :]
    ulast_ref[0] = last8
    ubuf_ref[0:8, :] = last8


def _const_spec(shape):
    nd = len(shape)
    return pl.BlockSpec(shape, lambda *_: (0,) * nd, pipeline_mode=pl.Buffered(1))


def _tok_a(x, cprev, wg, wu, wd, g, b, win, cw, *, alpha, tm, seg_len):
    n, d = x.shape
    d_conv = cw.shape[1]
    n_tiles = n // tm
    row = lambda w: pl.BlockSpec((tm, w), lambda i: (i, 0))
    nq, nkv, nqi = N_HEADS * HEAD_DIM, N_KV_HEADS * HEAD_DIM, IDX_HEADS * IDX_DIM
    tiles_per_seg = seg_len // tm
    kern = functools.partial(_tok_a_kernel, alpha=alpha, tiles_per_seg=tiles_per_seg,
                             f_chunk=FFN_CHUNK, d_conv=d_conv)
    return pl.pallas_call(
        kern,
        grid=(n_tiles,),
        in_specs=[row(d),
                  pl.BlockSpec((1, 8, d_conv), lambda i: (i // tiles_per_seg, 0, 0)),
                  _const_spec(wg.shape), _const_spec(wu.shape), _const_spec(wd.shape),
                  _const_spec(g.shape), _const_spec(b.shape), _const_spec(win.shape),
                  _const_spec(cw.shape)],
        out_specs=[row(d), row(d_conv), row(nq), row(nkv), row(nkv), row(nqi), row(LANES),
                   pl.BlockSpec((1, 8, d_conv), lambda i: (i, 0, 0))],
        out_shape=[jax.ShapeDtypeStruct((n, d), F32),
                   jax.ShapeDtypeStruct((n, d_conv), F32),
                   jax.ShapeDtypeStruct((n, nq), BF16),
                   jax.ShapeDtypeStruct((n, nkv), F32),
                   jax.ShapeDtypeStruct((n, nkv), F32),
                   jax.ShapeDtypeStruct((n, nqi), BF16),
                   jax.ShapeDtypeStruct((n, LANES), F32),
                   jax.ShapeDtypeStruct((n_tiles, 8, d_conv), F32)],
        scratch_shapes=[pltpu.VMEM((tm, d), F32), pltpu.VMEM((tm + 8, d_conv), F32)],
        compiler_params=pltpu.CompilerParams(dimension_semantics=("arbitrary",),
                                             vmem_limit_bytes=VMEM_LIMIT),
        name="tok_a",
    )(x, cprev, wg, wu, wd, g, b, win, cw)


def _memkv_kernel(m_ref, w_ref, o_ref):
    o_ref[0] = _dot(m_ref[...].astype(BF16), w_ref[0])


def _memkv(mem, w2):
    n, d = mem.shape
    dout = w2.shape[2]
    return pl.pallas_call(
        _memkv_kernel,
        grid=(2,),
        in_specs=[pl.BlockSpec((n, d), lambda j: (0, 0)),
                  pl.BlockSpec((1, d, dout), lambda j: (j, 0, 0))],
        out_specs=pl.BlockSpec((1, n, dout), lambda j: (j, 0, 0)),
        out_shape=jax.ShapeDtypeStruct((2, n, dout), F32),
        compiler_params=pltpu.CompilerParams(dimension_semantics=("arbitrary",),
                                             vmem_limit_bytes=VMEM_LIMIT),
        name="memkv",
    )(mem, w2)


def _float_to_key(x):
    b = lax.bitcast_convert_type(x, I32)
    return jnp.where(b >= 0, b, b ^ 0x7FFFFFFF)


def _key_to_float(key):
    bits = jnp.where(key >= 0, key, key ^ 0x7FFFFFFF)
    return lax.bitcast_convert_type(bits, F32)


def _lane_fold(m, op):
    part = m[:, 0:LANES]
    for c in range(1, m.shape[1] // LANES):
        part = op(part, m[:, c * LANES:(c + 1) * LANES])
    return part


def _admissible_keys(pos0, q0, tq, s_real):
    return jnp.minimum(((pos0 + q0 + tq - 1) // CHUNK + 1) * CHUNK, s_real)


def _dsa_score_kernel(qi_ref, wi_ref, kit_ref, sc_ref, rmax_ref, qis_ref, wb_ref,
                      *, pos0, s_real, ts, rows, wi_scale):
    tq = qi_ref.shape[0]
    row_blocks = [slice(r * rows, (r + 1) * rows) for r in range(tq // rows)]
    q0 = pl.program_id(1) * tq
    nkt = (_admissible_keys(pos0, q0, tq, s_real) + ts - 1) // ts

    def lane_tile(a):
        return jnp.concatenate([a] * (ts // LANES), axis=1)

    wi = wi_ref[...] * wi_scale
    for h in range(IDX_HEADS):
        qis_ref[h] = qi_ref[:, h * IDX_DIM:(h + 1) * IDX_DIM]
        wb_ref[h] = jnp.broadcast_to(wi[:, h:h + 1], (tq, LANES))
    q_chunk = lax.shift_right_arithmetic(
        pos0 + q0 + lax.broadcasted_iota(I32, (tq, 1), 0), 6)
    rmax_ref[...] = jnp.full(rmax_ref.shape, -jnp.inf, F32)

    def score_body(j, carry):
        off = pl.multiple_of(j * ts, ts)
        kit = kit_ref[:, pl.ds(off, ts)]
        kidx = j * ts + lax.broadcasted_iota(I32, (1, ts), 1)
        k_chunk = lax.shift_right_arithmetic(kidx, 6)
        for rs in row_blocks:
            acc = None
            for h in range(IDX_HEADS):
                term = lane_tile(wb_ref[h, rs, :]) * jnp.maximum(_dot(qis_ref[h, rs, :], kit), 0.0)
                acc = term if acc is None else acc + term
            admissible = (k_chunk <= q_chunk[rs]) & (kidx < s_real)
            s = jnp.where(admissible, acc, -jnp.inf)
            sc_ref[rs, pl.ds(off, ts)] = s
            rmax_ref[rs, :] = jnp.maximum(rmax_ref[rs, :], _lane_fold(s, jnp.maximum))
        return carry

    lax.fori_loop(0, nkt, score_body, 0)

    def fill_body(j, carry):
        sc_ref[:, pl.ds(pl.multiple_of(j * ts, ts), ts)] = jnp.full((tq, ts), -jnp.inf, F32)
        return carry

    lax.fori_loop(nkt, sc_ref.shape[1] // ts, fill_body, 0)


def _dsa_select_kernel(sc_ref, rmax_ref, bias_ref, *, pos0, s_real, ts, top_k):
    tq = sc_ref.shape[0]
    q0 = pl.program_id(1) * tq
    nkt = (_admissible_keys(pos0, q0, tq, s_real) + ts - 1) // ts
    k_f = float(top_k)

    def tile_off(j):
        return pl.multiple_of(j * ts, ts)

    def key_index(j):
        return j * ts + lax.broadcasted_iota(I32, (1, ts), 1)

    def count_rows(pred):
        def body(j, a):
            s = sc_ref[:, pl.ds(tile_off(j), ts)]
            return a + _lane_fold(jnp.where(pred(s, j), 1.0, 0.0), jnp.add)
        a = lax.fori_loop(0, nkt, body, jnp.zeros((tq, LANES), F32))
        return jnp.sum(a, axis=1, keepdims=True)

    def any_row(flag):
        return jnp.max(jnp.where(flag, 1.0, 0.0)) > 0.5

    def key_mid(lo, hi):
        return (lax.shift_right_arithmetic(lo, 1) + lax.shift_right_arithmetic(hi, 1)
                + (lo & hi & 1))

    def probe(lo, hi, cnt_lo, mid):
        thr = _key_to_float(mid)
        cnt = count_rows(lambda s, j: s >= thr)
        ge = cnt >= k_f
        exact = cnt == k_f
        lo_n = jnp.where(ge, mid, lo)
        hi_n = jnp.where(exact, mid + 1, jnp.where(ge, hi, mid))
        return lo_n, hi_n, jnp.where(ge, cnt, cnt_lo)

    def two_probes(c):
        c = probe(*c, key_mid(c[0], c[1]))
        return probe(*c, key_mid(c[0], c[1]))

    row_max = jnp.max(rmax_ref[...], axis=1, keepdims=True)
    lo0 = jnp.full((tq, 1), KEY_NEG_INF, I32)
    hi0 = _float_to_key(row_max) + 1
    cnt0 = jnp.zeros((tq, 1), F32) + (nkt * ts).astype(F32)
    mid0 = jnp.where(row_max > 0.0, _float_to_key(row_max * 0.25), key_mid(lo0, hi0))
    lo, _, cnt_lo = lax.while_loop(lambda c: any_row(c[0] + 1 < c[1]), two_probes,
                                   probe(lo0, hi0, cnt0, mid0))
    thr = _key_to_float(lo)
    thr_fin = jnp.maximum(thr, F32_LOWEST)
    excess = jnp.where(thr == -jnp.inf, 0.0, cnt_lo - k_f)
    tied = any_row(excess > 0.5)

    @pl.when(tied)
    def _():
        def tie_cond(c):
            jl, jh = c
            return any_row(jl + 1 < jh)

        def tie_body(c):
            jl, jh = c
            mid = lax.shift_right_arithmetic(jl + jh, 1)
            cnt = count_rows(lambda s, j: (s == thr) & (key_index(j) >= mid))
            ge = cnt >= excess
            return jnp.where(ge, mid, jl), jnp.where(ge, jh, mid)

        jl, _ = lax.while_loop(tie_cond, tie_body,
                               (jnp.zeros((tq, 1), I32), jnp.zeros((tq, 1), I32) + nkt * ts))

        def bias_body(j, carry):
            off = tile_off(j)
            s = sc_ref[:, pl.ds(off, ts)]
            drop = (s == thr) & (key_index(j) >= jl) & (excess > 0.5)
            bias_ref[:, pl.ds(off, ts)] = jnp.where((s >= thr_fin) & ~drop, 0.0, NEG_BIG)
            return carry

        lax.fori_loop(0, nkt, bias_body, 0)

    @pl.when(jnp.logical_not(tied))
    def _():
        def bias_body(j, carry):
            off = tile_off(j)
            s = sc_ref[:, pl.ds(off, ts)]
            bias_ref[:, pl.ds(off, ts)] = jnp.where(s >= thr_fin, 0.0, NEG_BIG)
            return carry

        lax.fori_loop(0, nkt, bias_body, 0)

    def fill_body(j, carry):
        bias_ref[:, pl.ds(tile_off(j), ts)] = jnp.full((tq, ts), NEG_BIG, F32)
        return carry

    lax.fori_loop(nkt, bias_ref.shape[1] // ts, fill_body, 0)


def _dsa_attend_kernel(q_ref, bias_ref, kt_ref, v_ref, o_ref, qs_ref, m_ref, acc_ref,
                       *, pos0, s_real, ts, rows, hs):
    tq = q_ref.shape[0]
    row_blocks = [slice(r * rows, (r + 1) * rows) for r in range(tq // rows)]
    n_sets = N_HEADS // hs
    q0 = pl.program_id(1) * tq
    nkt = (_admissible_keys(pos0, q0, tq, s_real) + ts - 1) // ts

    def lane_tile(a):
        return jnp.concatenate([a] * (ts // LANES), axis=1)

    def stacked(rs):
        return slice(rs.start * hs, rs.stop * hs)

    io_blocks = row_blocks if hs > 1 else [slice(0, tq)]
    for st in range(n_sets):
        for rs in io_blocks:
            qs_ref[st, stacked(rs), :] = jnp.concatenate(
                [q_ref[rs, h * HEAD_DIM:(h + 1) * HEAD_DIM] for h in range(st * hs, (st + 1) * hs)],
                axis=0)
    m_ref[...] = jnp.full(m_ref.shape, NEG_BIG, F32)
    acc_ref[...] = jnp.zeros(acc_ref.shape, F32)

    def att_body(j, carry):
        off = pl.multiple_of(j * ts, ts)
        for rs in row_blocks:
            bias = jnp.concatenate([bias_ref[rs, pl.ds(off, ts)]] * hs, axis=0)
            srs = stacked(rs)
            for st in range(n_sets):
                g = (st * hs) // GROUP
                s = _dot(qs_ref[st, srs, :], kt_ref[g, :, pl.ds(off, ts)]) + bias
                m_old = m_ref[st, srs, :]
                m_new = jnp.maximum(m_old, jnp.max(s, axis=1, keepdims=True))
                p = jnp.exp(s - lane_tile(m_new)).astype(BF16)
                acc_ref[st, srs, :] = (jnp.exp(m_old - m_new) * acc_ref[st, srs, :]
                                       + _dot(p, v_ref[g, pl.ds(off, ts), :]))
                m_ref[st, srs, :] = m_new
        return carry

    lax.fori_loop(0, nkt, att_body, 0)
    for st in range(n_sets):
        for rs in io_blocks:
            n_rows = rs.stop - rs.start
            for i in range(hs):
                h = st * hs + i
                part = slice(rs.start * hs + i * n_rows, rs.start * hs + (i + 1) * n_rows)
                acc = acc_ref[st, part, :]
                o_ref[rs, h * HEAD_DIM:(h + 1) * HEAD_DIM] = (
                    acc[:, 0:HEAD_DIM] / acc[:, HEAD_DIM:HEAD_DIM + 1])


def _dsa(q, qi, wi, kit, kt, v, *, pos0, s_real, tq_sel, tq_att, rows, ts, top_k):
    nb, t, nq = q.shape
    s_pad = kit.shape[2]
    assert s_pad % ts == 0 and t % tq_sel == 0 and t % tq_att == 0 and tq_att % rows == 0
    assert top_k <= ts and top_k <= s_real
    params = pltpu.CompilerParams(dimension_semantics=("arbitrary", "arbitrary"),
                                  vmem_limit_bytes=VMEM_LIMIT)
    score = functools.partial(_dsa_score_kernel, pos0=pos0, s_real=s_real, ts=ts, rows=rows,
                              wi_scale=(IDX_HEADS * IDX_DIM) ** -0.5)
    scores, row_max = pl.pallas_call(
        score,
        grid=(nb, t // tq_att),
        in_specs=[pl.BlockSpec((None, tq_att, qi.shape[2]), lambda b, i: (b, i, 0)),
                  pl.BlockSpec((None, tq_att, wi.shape[2]), lambda b, i: (b, i, 0)),
                  pl.BlockSpec((None, IDX_DIM, s_pad), lambda b, i: (b, 0, 0),
                               pipeline_mode=pl.Buffered(1))],
        out_specs=[pl.BlockSpec((None, tq_att, s_pad), lambda b, i: (b, i, 0)),
                   pl.BlockSpec((None, tq_att, LANES), lambda b, i: (b, i, 0))],
        out_shape=[jax.ShapeDtypeStruct((nb, t, s_pad), F32),
                   jax.ShapeDtypeStruct((nb, t, LANES), F32)],
        scratch_shapes=[pltpu.VMEM((IDX_HEADS, tq_att, IDX_DIM), BF16),
                        pltpu.VMEM((IDX_HEADS, tq_att, LANES), F32)],
        compiler_params=params,
        name="dsa_score",
    )(qi, wi, kit)
    sel = functools.partial(_dsa_select_kernel, pos0=pos0, s_real=s_real, ts=ts, top_k=top_k)
    bias = pl.pallas_call(
        sel,
        grid=(nb, t // tq_sel),
        in_specs=[pl.BlockSpec((None, tq_sel, s_pad), lambda b, i: (b, i, 0)),
                  pl.BlockSpec((None, tq_sel, LANES), lambda b, i: (b, i, 0))],
        out_specs=pl.BlockSpec((None, tq_sel, s_pad), lambda b, i: (b, i, 0)),
        out_shape=jax.ShapeDtypeStruct((nb, t, s_pad), F32),
        compiler_params=params,
        name="dsa_select",
    )(scores, row_max)
    hs = max(1, min(GROUP, DSA_ATTEND_BLOCK // rows))
    att = functools.partial(_dsa_attend_kernel, pos0=pos0, s_real=s_real, ts=ts, rows=rows, hs=hs)
    tile3 = lambda w, dt: pltpu.VMEM((N_HEADS // hs, hs * tq_att, w), dt)
    return pl.pallas_call(
        att,
        grid=(nb, t // tq_att),
        in_specs=[pl.BlockSpec((None, tq_att, nq), lambda b, i: (b, i, 0)),
                  pl.BlockSpec((None, tq_att, s_pad), lambda b, i: (b, i, 0)),
                  pl.BlockSpec((None, N_KV_HEADS, HEAD_DIM, s_pad), lambda b, i: (b, 0, 0, 0),
                               pipeline_mode=pl.Buffered(1)),
                  pl.BlockSpec((None, N_KV_HEADS, s_pad, LANES), lambda b, i: (b, 0, 0, 0),
                               pipeline_mode=pl.Buffered(1))],
        out_specs=pl.BlockSpec((None, tq_att, nq), lambda b, i: (b, i, 0)),
        out_shape=jax.ShapeDtypeStruct((nb, t, nq), F32),
        scratch_shapes=[tile3(HEAD_DIM, BF16), tile3(LANES, F32), tile3(LANES, F32)],
        compiler_params=params,
        name="dsa_attend",
    )(q, bias, kt, v)


def _tok_b_kernel(x1_ref, yc_ref, ya_ref, mk_ref, mv_ref, wmo_ref, g2_ref, b2_ref,
                  wq_ref, wo_ref, g3_ref, b3_ref, wg_ref, wu_ref, wd_ref, g4_ref, b4_ref,
                  y_ref, acc_ref, *, alpha, f_chunk):
    x1 = x1_ref[...]
    mixed = jnp.concatenate([yc_ref[...], ya_ref[...]], axis=-1).astype(BF16)
    x2 = _layer_norm(alpha * x1 + _dot(mixed, wmo_ref[...]), g2_ref[...], b2_ref[...])

    d = x2.shape[1]
    dh = d // MEM_HEADS
    qm = (_dot(x2.astype(BF16), wq_ref[...]) * (dh ** -0.5)).astype(BF16)
    mk = mk_ref[...].astype(BF16)
    mv = mv_ref[...].astype(BF16)
    heads = []
    for h in range(MEM_HEADS):
        sl = slice(h * dh, (h + 1) * dh)
        s = _dot_nt(qm[:, sl], mk[:, sl])
        e = jnp.exp(s - jnp.max(s, axis=-1, keepdims=True))
        p = e / jnp.sum(e, axis=-1, keepdims=True)
        heads.append(_dot(p.astype(BF16), mv[:, sl]))
    o = jnp.concatenate(heads, axis=-1).astype(BF16)
    x3 = _layer_norm(alpha * x2 + _dot(o, wo_ref[...]), g3_ref[...], b3_ref[...])

    _swiglu_into(acc_ref, x3.astype(BF16), wg_ref, wu_ref, wd_ref, f_chunk)
    y_ref[...] = _layer_norm(alpha * x3 + 0.5 * acc_ref[...], g4_ref[...], b4_ref[...])


def _tok_b(x1, yc, ya, mk, mv, wmo, g2, b2, wq, wo, g3, b3, wg, wu, wd, g4, b4,
           *, alpha, tm, seg_len):
    n, d = x1.shape
    n_mem = mk.shape[1]
    tiles_per_seg = seg_len // tm
    row = lambda w: pl.BlockSpec((tm, w), lambda i: (i, 0))
    mem = pl.BlockSpec((None, n_mem, d), lambda i: (i // tiles_per_seg, 0, 0))
    consts = [wmo, g2, b2, wq, wo, g3, b3, wg, wu, wd, g4, b4]
    kern = functools.partial(_tok_b_kernel, alpha=alpha, f_chunk=FFN_CHUNK)
    return pl.pallas_call(
        kern,
        grid=(n // tm,),
        in_specs=[row(d), row(yc.shape[1]), row(ya.shape[1]), mem, mem]
                 + [_const_spec(c.shape) for c in consts],
        out_specs=row(d),
        out_shape=jax.ShapeDtypeStruct((n, d), F32),
        scratch_shapes=[pltpu.VMEM((tm, d), F32)],
        compiler_params=pltpu.CompilerParams(dimension_semantics=("arbitrary",),
                                             vmem_limit_bytes=VMEM_LIMIT),
        name="tok_b",
    )(x1, yc, ya, mk, mv, *consts)


def _round_up(x, m):
    return (x + m - 1) // m * m


def _layer(x, conv_prev, k_past, v_past, ik_past, mem_k, mem_v, p, *, alpha):
    b, t, d = x.shape
    past = k_past.shape[1]
    d_conv = p["conv_w"].shape[1]
    top_k = min(TOPK_MAX, (past + t) // 4)
    tm = min(TOKEN_TILE, t)
    ts = DSA_KEY_TILE

    cprev = jnp.pad(conv_prev, ((0, 0), (6, 0), (0, 0)))
    x1, yconv, q, k, v, qi, tail, ulast = _tok_a(
        x.reshape(b * t, d), cprev, p["ffn1_gate"], p["ffn1_up"], p["ffn1_down"],
        p["ln1_g"], p["ln1_b"], p["w_mix_in"], p["conv_w"], alpha=alpha, tm=tm, seg_len=t)
    new_conv = ulast.reshape(b, t // tm, 8, d_conv)[:, -1, 6:8, :]
    k = k.reshape(b, t, N_KV_HEADS, HEAD_DIM)
    v = v.reshape(b, t, N_KV_HEADS, HEAD_DIM)
    ki = tail[:, :IDX_DIM].reshape(b, t, IDX_DIM)
    wi = tail[:, IDX_DIM:IDX_DIM + IDX_HEADS].reshape(b, t, IDX_HEADS)

    s_real = past + t
    s_pad = _round_up(s_real, ts)
    pad = lambda a: jnp.pad(a, ((0, 0), (0, s_pad - s_real)) + ((0, 0),) * (a.ndim - 2))
    k_all = pad(jnp.concatenate([k_past, k], axis=1)).astype(BF16)
    v_all = pad(jnp.concatenate([v_past, v], axis=1)).astype(BF16)
    ki_all = pad(jnp.concatenate([ik_past, ki], axis=1)).astype(BF16)
    ones_col = (jnp.arange(LANES - HEAD_DIM) == 0).astype(BF16)
    v_aug = jnp.concatenate(
        [v_all.transpose(0, 2, 1, 3),
         jnp.broadcast_to(ones_col, (b, N_KV_HEADS, s_pad, LANES - HEAD_DIM))], axis=-1)
    y_attn = _dsa(q.reshape(b, t, -1), qi.reshape(b, t, -1), wi,
                  ki_all.transpose(0, 2, 1), k_all.transpose(0, 2, 3, 1), v_aug,
                  pos0=past, s_real=s_real, tq_sel=min(DSA_SELECT_ROWS, t),
                  tq_att=min(DSA_ATTEND_ROWS, t), rows=min(DSA_ATTEND_BLOCK, t), ts=ts, top_k=top_k)

    y = _tok_b(x1, yconv, y_attn.reshape(b * t, -1), mem_k.reshape(b, mem_k.shape[1], d),
               mem_v.reshape(b, mem_v.shape[1], d),
               p["w_mix_out"], p["ln2_g"], p["ln2_b"], p["w_mem_q"], p["w_mem_o"],
               p["ln3_g"], p["ln3_b"], p["ffn2_gate"], p["ffn2_up"], p["ffn2_down"],
               p["ln4_g"], p["ln4_b"], alpha=alpha, tm=tm, seg_len=t)
    return y.reshape(b, t, d), new_conv, k, v, ki


def kernel(x_prompt, x_sample, cache_conv, cache_k, cache_v, cache_idx_k, cache_mem_k, cache_mem_v,
           mem_prompt, ffn1_gate, ffn1_up, ffn1_down, ln1_g, ln1_b, w_mix_in, conv_w, w_mix_out,
           ln2_g, ln2_b, w_mem_q, w_mem_k, w_mem_v, w_mem_o, ln3_g, ln3_b,
           ffn2_gate, ffn2_up, ffn2_down, ln4_g, ln4_b):
    depth = ffn1_gate.shape[0]
    alpha = (2.0 * depth) ** 0.25
    b_p, t_p, d = x_prompt.shape
    dt = x_prompt.dtype
    d_conv = conv_w.shape[2]
    n_mem = mem_prompt.shape[1]
    bf = lambda w: w.astype(BF16)
    vec = lambda a: a.reshape(1, -1)

    y_p, y_s = x_prompt, x_sample
    outs_p = [[] for _ in range(6)]
    outs_s = [[] for _ in range(4)]
    for l in range(depth):
        d_in = w_mix_in.shape[2]
        p = dict(ffn1_gate=bf(ffn1_gate[l]), ffn1_up=bf(ffn1_up[l]), ffn1_down=bf(ffn1_down[l]),
                 ln1_g=vec(ln1_g[l]), ln1_b=vec(ln1_b[l]),
                 w_mix_in=jnp.pad(bf(w_mix_in[l]),
                                  ((0, 0), (0, _round_up(d_in, LANES) - d_in))),
                 conv_w=conv_w[l], w_mix_out=bf(w_mix_out[l]),
                 ln2_g=vec(ln2_g[l]), ln2_b=vec(ln2_b[l]),
                 w_mem_q=bf(w_mem_q[l]), w_mem_o=bf(w_mem_o[l]),
                 ln3_g=vec(ln3_g[l]), ln3_b=vec(ln3_b[l]),
                 ffn2_gate=bf(ffn2_gate[l]), ffn2_up=bf(ffn2_up[l]), ffn2_down=bf(ffn2_down[l]),
                 ln4_g=vec(ln4_g[l]), ln4_b=vec(ln4_b[l]))
        mem_kv = _memkv(mem_prompt.reshape(b_p * n_mem, d),
                        jnp.stack([bf(w_mem_k[l]), bf(w_mem_v[l])]))
        mem_k = mem_kv[0].reshape(b_p, n_mem, MEM_HEADS, d // MEM_HEADS)
        mem_v = mem_kv[1].reshape(b_p, n_mem, MEM_HEADS, d // MEM_HEADS)
        y_p, c_new, k_new, v_new, ik_new = _layer(
            y_p, jnp.zeros((b_p, 2, d_conv), dt),
            jnp.zeros((b_p, 0, N_KV_HEADS, HEAD_DIM), dt), jnp.zeros((b_p, 0, N_KV_HEADS, HEAD_DIM), dt),
            jnp.zeros((b_p, 0, IDX_DIM), dt), mem_k, mem_v, p, alpha=alpha)
        for lst, a in zip(outs_p, (c_new, k_new, v_new, ik_new, mem_k, mem_v)):
            lst.append(a)
        y_s, c_new, k_new, v_new, ik_new = _layer(
            y_s, cache_conv[l], cache_k[l], cache_v[l], cache_idx_k[l],
            cache_mem_k[l], cache_mem_v[l], p, alpha=alpha)
        for lst, a in zip(outs_s, (c_new, k_new, v_new, ik_new)):
            lst.append(a)
    return (y_p, y_s, *[jnp.stack(o) for o in outs_p], *[jnp.stack(o) for o in outs_s])
```

```python
import functools

import jax
import jax.numpy as jnp
from jax import lax
from jax.experimental import pallas as pl
from jax.experimental.pallas import tpu as pltpu

F32 = jnp.float32
BF16 = jnp.bfloat16
I32 = jnp.int32

CHUNK = 64
N_HEADS = 8
HEAD_DIM = 64
N_KV_HEADS = 2
GROUP = N_HEADS // N_KV_HEADS
IDX_HEADS = 8
IDX_DIM = 64
TOPK_MAX = 256
MEM_HEADS = 4
LN_EPS = 1e-5

LANES = 128
VMEM_LIMIT = 56 * 1024 * 1024
NEG_BIG = -1e30
F32_LOWEST = -3.4028234663852886e38
KEY_NEG_INF = -2139095041

TOKEN_TILE = 512
DSA_SELECT_ROWS = 128
DSA_ATTEND_ROWS = 512
DSA_ATTEND_BLOCK = 128
DSA_KEY_TILE = 256
FFN_CHUNK = 256


def _dot(a, b):
    return jnp.dot(a, b, preferred_element_type=F32)


def _dot_nt(a, b):
    return lax.dot_general(a, b, (((1,), (1,)), ((), ())), preferred_element_type=F32)


def _layer_norm(z, g, b):
    mu = jnp.mean(z, axis=-1, keepdims=True)
    d = z - mu
    var = jnp.mean(d * d, axis=-1, keepdims=True)
    return d * lax.rsqrt(var + LN_EPS) * g + b


def _swiglu_into(acc_ref, xb, wg_ref, wu_ref, wd_ref, f_chunk):
    d_ff = wg_ref.shape[1]
    for c in range(d_ff // f_chunk):
        sl = slice(c * f_chunk, (c + 1) * f_chunk)
        g = _dot(xb, wg_ref[:, sl])
        u = _dot(xb, wu_ref[:, sl])
        h = (g * (1.0 / (1.0 + jnp.exp(-g)))) * u
        part = _dot(h.astype(BF16), wd_ref[sl, :])
        if c == 0:
            acc_ref[...] = part
        else:
            acc_ref[...] += part


def _tok_a_kernel(x_ref, cprev_ref, wg_ref, wu_ref, wd_ref, g_ref, b_ref, win_ref, cw_ref,
                  x1_ref, yconv_ref, q_ref, k_ref, v_ref, qi_ref, tail_ref, ulast_ref,
                  acc_ref, ubuf_ref, *, alpha, tiles_per_seg, f_chunk, d_conv):
    i = pl.program_id(0)
    tm = x_ref.shape[0]
    x = x_ref[...]
    _swiglu_into(acc_ref, x.astype(BF16), wg_ref, wu_ref, wd_ref, f_chunk)
    x1 = _layer_norm(alpha * x + 0.5 * acc_ref[...], g_ref[...], b_ref[...])
    x1_ref[...] = x1
    x1b = x1.astype(BF16)

    c0 = 0
    h = _dot(x1b, win_ref[:, c0:c0 + d_conv]); c0 += d_conv
    gate_b = _dot(x1b, win_ref[:, c0:c0 + d_conv]); c0 += d_conv
    gate_c = _dot(x1b, win_ref[:, c0:c0 + d_conv]); c0 += d_conv
    nq = N_HEADS * HEAD_DIM
    q_ref[...] = (_dot(x1b, win_ref[:, c0:c0 + nq]) * (HEAD_DIM ** -0.5)).astype(BF16); c0 += nq
    nkv = N_KV_HEADS * HEAD_DIM
    k_ref[...] = _dot(x1b, win_ref[:, c0:c0 + nkv]); c0 += nkv
    v_ref[...] = _dot(x1b, win_ref[:, c0:c0 + nkv]); c0 += nkv
    nqi = IDX_HEADS * IDX_DIM
    qi_ref[...] = _dot(x1b, win_ref[:, c0:c0 + nqi]).astype(BF16); c0 += nqi
    tail_ref[...] = _dot(x1b, win_ref[:, c0:c0 + LANES])

    u = gate_c * h

    @pl.when(i % tiles_per_seg == 0)
    def _():
        ubuf_ref[0:8, :] = cprev_ref[0]

    ubuf_ref[8:tm + 8, :] = u
    um1 = ubuf_ref[7:tm + 7, :]
    um2 = ubuf_ref[6:tm + 6, :]
    cw = cw_ref[...]
    conv = cw[0:1, :] * um2 + cw[1:2, :] * um1 + cw[2:3, :] * u
    yconv_ref[...] = gate_b * conv
    last8 = ubuf_ref[tm:tm + 8, :]
    ulast_ref[0] = last8
    ubuf_ref[0:8, :] = last8


def _const_spec(shape):
    nd = len(shape)
    return pl.BlockSpec(shape, lambda *_: (0,) * nd, pipeline_mode=pl.Buffered(1))


def _tok_a(x, cprev, wg, wu, wd, g, b, win, cw, *, alpha, tm, seg_len):
    n, d = x.shape
    d_conv = cw.shape[1]
    n_tiles = n // tm
    row = lambda w: pl.BlockSpec((tm, w), lambda i: (i, 0))
    nq, nkv, nqi = N_HEADS * HEAD_DIM, N_KV_HEADS * HEAD_DIM, IDX_HEADS * IDX_DIM
    tiles_per_seg = seg_len // tm
    kern = functools.partial(_tok_a_kernel, alpha=alpha, tiles_per_seg=tiles_per_seg,
                             f_chunk=FFN_CHUNK, d_conv=d_conv)
    return pl.pallas_call(
        kern,
        grid=(n_tiles,),
        in_specs=[row(d),
                  pl.BlockSpec((1, 8, d_conv), lambda i: (i // tiles_per_seg, 0, 0)),
                  _const_spec(wg.shape), _const_spec(wu.shape), _const_spec(wd.shape),
                  _const_spec(g.shape), _const_spec(b.shape), _const_spec(win.shape),
                  _const_spec(cw.shape)],
        out_specs=[row(d), row(d_conv), row(nq), row(nkv), row(nkv), row(nqi), row(LANES),
                   pl.BlockSpec((1, 8, d_conv), lambda i: (i, 0, 0))],
        out_shape=[jax.ShapeDtypeStruct((n, d), F32),
                   jax.ShapeDtypeStruct((n, d_conv), F32),
                   jax.ShapeDtypeStruct((n, nq), BF16),
                   jax.ShapeDtypeStruct((n, nkv), F32),
                   jax.ShapeDtypeStruct((n, nkv), F32),
                   jax.ShapeDtypeStruct((n, nqi), BF16),
                   jax.ShapeDtypeStruct((n, LANES), F32),
                   jax.ShapeDtypeStruct((n_tiles, 8, d_conv), F32)],
        scratch_shapes=[pltpu.VMEM((tm, d), F32), pltpu.VMEM((tm + 8, d_conv), F32)],
        compiler_params=pltpu.CompilerParams(dimension_semantics=("arbitrary",),
                                             vmem_limit_bytes=VMEM_LIMIT),
        name="tok_a",
    )(x, cprev, wg, wu, wd, g, b, win, cw)


def _memkv_kernel(m_ref, w_ref, o_ref):
    o_ref[0] = _dot(m_ref[...].astype(BF16), w_ref[0])


def _memkv(mem, w2):
    n, d = mem.shape
    dout = w2.shape[2]
    return pl.pallas_call(
        _memkv_kernel,
        grid=(2,),
        in_specs=[pl.BlockSpec((n, d), lambda j: (0, 0)),
                  pl.BlockSpec((1, d, dout), lambda j: (j, 0, 0))],
        out_specs=pl.BlockSpec((1, n, dout), lambda j: (j, 0, 0)),
        out_shape=jax.ShapeDtypeStruct((2, n, dout), F32),
        compiler_params=pltpu.CompilerParams(dimension_semantics=("arbitrary",),
                                             vmem_limit_bytes=VMEM_LIMIT),
        name="memkv",
    )(mem, w2)


def _float_to_key(x):
    b = lax.bitcast_convert_type(x, I32)
    return jnp.where(b >= 0, b, b ^ 0x7FFFFFFF)


def _key_to_float(key):
    bits = jnp.where(key >= 0, key, key ^ 0x7FFFFFFF)
    return lax.bitcast_convert_type(bits, F32)


def _lane_fold(m, op):
    part = m[:, 0:LANES]
    for c in range(1, m.shape[1] // LANES):
        part = op(part, m[:, c * LANES:(c + 1) * LANES])
    return part


def _admissible_keys(pos0, q0, tq, s_real):
    return jnp.minimum(((pos0 + q0 + tq - 1) // CHUNK + 1) * CHUNK, s_real)


def _dsa_score_kernel(qi_ref, wi_ref, kit_ref, sc_ref, rmax_ref, qis_ref, wb_ref,
                      *, pos0, s_real, ts, rows, hs, wi_scale):
    tq = qi_ref.shape[0]
    row_blocks = [slice(r * rows, (r + 1) * rows) for r in range(tq // rows)]
    q0 = pl.program_id(1) * tq
    nkt = (_admissible_keys(pos0, q0, tq, s_real) + ts - 1) // ts

    def lane_tile(a):
        return jnp.concatenate([a] * (ts // LANES), axis=1)

    wi = wi_ref[...] * wi_scale
    for h in range(IDX_HEADS):
        wb_ref[h] = jnp.broadcast_to(wi[:, h:h + 1], (tq, LANES))
    for st in range(IDX_HEADS // hs):
        for rs in row_blocks:
            qis_ref[st, rs.start * hs:rs.stop * hs, :] = jnp.concatenate(
                [qi_ref[rs, h * IDX_DIM:(h + 1) * IDX_DIM] for h in range(st * hs, (st + 1) * hs)],
                axis=0)
    q_chunk = lax.shift_right_arithmetic(
        pos0 + q0 + lax.broadcasted_iota(I32, (tq, 1), 0), 6)
    rmax_ref[...] = jnp.full(rmax_ref.shape, -jnp.inf, F32)

    def score_body(j, carry):
        off = pl.multiple_of(j * ts, ts)
        kit = kit_ref[:, pl.ds(off, ts)]
        kidx = j * ts + lax.broadcasted_iota(I32, (1, ts), 1)
        k_chunk = lax.shift_right_arithmetic(kidx, 6)
        for rs in row_blocks:
            acc = None
            for st in range(IDX_HEADS // hs):
                logits = _dot(qis_ref[st, rs.start * hs:rs.stop * hs, :], kit)
                for i in range(hs):
                    h = st * hs + i
                    term = (lane_tile(wb_ref[h, rs, :])
                            * jnp.maximum(logits[i * rows:(i + 1) * rows, :], 0.0))
                    acc = term if acc is None else acc + term
            admissible = (k_chunk <= q_chunk[rs]) & (kidx < s_real)
            s = jnp.where(admissible, acc, -jnp.inf)
            sc_ref[rs, pl.ds(off, ts)] = s
            rmax_ref[rs, :] = jnp.maximum(rmax_ref[rs, :], _lane_fold(s, jnp.maximum))
        return carry

    lax.fori_loop(0, nkt, score_body, 0)

    def fill_body(j, carry):
        sc_ref[:, pl.ds(pl.multiple_of(j * ts, ts), ts)] = jnp.full((tq, ts), -jnp.inf, F32)
        return carry

    lax.fori_loop(nkt, sc_ref.shape[1] // ts, fill_body, 0)


def _dsa_select_kernel(sc_ref, rmax_ref, bias_ref, *, pos0, s_real, ts, top_k):
    tq = sc_ref.shape[0]
    q0 = pl.program_id(1) * tq
    nkt = (_admissible_keys(pos0, q0, tq, s_real) + ts - 1) // ts
    k_f = float(top_k)

    def tile_off(j):
        return pl.multiple_of(j * ts, ts)

    def key_index(j):
        return j * ts + lax.broadcasted_iota(I32, (1, ts), 1)

    def count_rows(pred):
        def body(j, a):
            s = sc_ref[:, pl.ds(tile_off(j), ts)]
            return a + _lane_fold(jnp.where(pred(s, j), 1.0, 0.0), jnp.add)
        a = lax.fori_loop(0, nkt, body, jnp.zeros((tq, LANES), F32))
        return jnp.sum(a, axis=1, keepdims=True)

    def any_row(flag):
        return jnp.max(jnp.where(flag, 1.0, 0.0)) > 0.5

    def key_mid(lo, hi):
        return (lax.shift_right_arithmetic(lo, 1) + lax.shift_right_arithmetic(hi, 1)
                + (lo & hi & 1))

    def probe(lo, hi, cnt_lo, mid):
        thr = _key_to_float(mid)
        cnt = count_rows(lambda s, j: s >= thr)
        ge = cnt >= k_f
        exact = cnt == k_f
        lo_n = jnp.where(ge, mid, lo)
        hi_n = jnp.where(exact, mid + 1, jnp.where(ge, hi, mid))
        return lo_n, hi_n, jnp.where(ge, cnt, cnt_lo)

    def two_probes(c):
        c = probe(*c, key_mid(c[0], c[1]))
        return probe(*c, key_mid(c[0], c[1]))

    row_max = jnp.max(rmax_ref[...], axis=1, keepdims=True)
    lo0 = jnp.full((tq, 1), KEY_NEG_INF, I32)
    hi0 = _float_to_key(row_max) + 1
    cnt0 = jnp.zeros((tq, 1), F32) + (nkt * ts).astype(F32)
    mid0 = jnp.where(row_max > 0.0, _float_to_key(row_max * 0.25), key_mid(lo0, hi0))
    lo, _, cnt_lo = lax.while_loop(lambda c: any_row(c[0] + 1 < c[1]), two_probes,
                                   probe(lo0, hi0, cnt0, mid0))
    thr = _key_to_float(lo)
    thr_fin = jnp.maximum(thr, F32_LOWEST)
    excess = jnp.where(thr == -jnp.inf, 0.0, cnt_lo - k_f)
    tied = any_row(excess > 0.5)

    @pl.when(tied)
    def _():
        def tie_cond(c):
            jl, jh = c
            return any_row(jl + 1 < jh)

        def tie_body(c):
            jl, jh = c
            mid = lax.shift_right_arithmetic(jl + jh, 1)
            cnt = count_rows(lambda s, j: (s == thr) & (key_index(j) >= mid))
            ge = cnt >= excess
            return jnp.where(ge, mid, jl), jnp.where(ge, jh, mid)

        jl, _ = lax.while_loop(tie_cond, tie_body,
                               (jnp.zeros((tq, 1), I32), jnp.zeros((tq, 1), I32) + nkt * ts))

        def bias_body(j, carry):
            off = tile_off(j)
            s = sc_ref[:, pl.ds(off, ts)]
            drop = (s == thr) & (key_index(j) >= jl) & (excess > 0.5)
            bias_ref[:, pl.ds(off, ts)] = jnp.where((s >= thr_fin) & ~drop, 0.0, NEG_BIG)
            return carry

        lax.fori_loop(0, nkt, bias_body, 0)

    @pl.when(jnp.logical_not(tied))
    def _():
        def bias_body(j, carry):
            off = tile_off(j)
            s = sc_ref[:, pl.ds(off, ts)]
            bias_ref[:, pl.ds(off, ts)] = jnp.where(s >= thr_fin, 0.0, NEG_BIG)
            return carry

        lax.fori_loop(0, nkt, bias_body, 0)

    def fill_body(j, carry):
        bias_ref[:, pl.ds(tile_off(j), ts)] = jnp.full((tq, ts), NEG_BIG, F32)
        return carry

    lax.fori_loop(nkt, bias_ref.shape[1] // ts, fill_body, 0)


def _dsa_attend_kernel(q_ref, bias_ref, kt_ref, v_ref, o_ref, qs_ref, m_ref, acc_ref,
                       *, pos0, s_real, ts, rows, hs):
    tq = q_ref.shape[0]
    row_blocks = [slice(r * rows, (r + 1) * rows) for r in range(tq // rows)]
    n_sets = N_HEADS // hs
    q0 = pl.program_id(1) * tq
    nkt = (_admissible_keys(pos0, q0, tq, s_real) + ts - 1) // ts

    def lane_tile(a):
        return jnp.concatenate([a] * (ts // LANES), axis=1)

    def stacked(rs):
        return slice(rs.start * hs, rs.stop * hs)

    io_blocks = row_blocks if hs > 1 else [slice(0, tq)]
    for st in range(n_sets):
        for rs in io_blocks:
            qs_ref[st, stacked(rs), :] = jnp.concatenate(
                [q_ref[rs, h * HEAD_DIM:(h + 1) * HEAD_DIM] for h in range(st * hs, (st + 1) * hs)],
                axis=0)
    m_ref[...] = jnp.full(m_ref.shape, NEG_BIG, F32)
    acc_ref[...] = jnp.zeros(acc_ref.shape, F32)

    def att_body(j, carry):
        off = pl.multiple_of(j * ts, ts)
        for rs in row_blocks:
            bias = jnp.concatenate([bias_ref[rs, pl.ds(off, ts)]] * hs, axis=0)
            srs = stacked(rs)
            for st in range(n_sets):
                g = (st * hs) // GROUP
                s = _dot(qs_ref[st, srs, :], kt_ref[g, :, pl.ds(off, ts)]) + bias
                m_old = m_ref[st, srs, :]
                m_new = jnp.maximum(m_old, jnp.max(s, axis=1, keepdims=True))
                p = jnp.exp(s - lane_tile(m_new)).astype(BF16)
                acc_ref[st, srs, :] = (jnp.exp(m_old - m_new) * acc_ref[st, srs, :]
                                       + _dot(p, v_ref[g, pl.ds(off, ts), :]))
                m_ref[st, srs, :] = m_new
        return carry

    lax.fori_loop(0, nkt, att_body, 0)
    for st in range(n_sets):
        for rs in io_blocks:
            n_rows = rs.stop - rs.start
            for i in range(hs):
                h = st * hs + i
                part = slice(rs.start * hs + i * n_rows, rs.start * hs + (i + 1) * n_rows)
                acc = acc_ref[st, part, :]
                o_ref[rs, h * HEAD_DIM:(h + 1) * HEAD_DIM] = (
                    acc[:, 0:HEAD_DIM] / acc[:, HEAD_DIM:HEAD_DIM + 1])


def _dsa(q, qi, wi, kit, kt, v, *, pos0, s_real, tq_sel, tq_att, rows, ts, top_k):
    nb, t, nq = q.shape
    s_pad = kit.shape[2]
    assert s_pad % ts == 0 and t % tq_sel == 0 and t % tq_att == 0 and tq_att % rows == 0
    assert top_k <= ts and top_k <= s_real
    params = pltpu.CompilerParams(dimension_semantics=("arbitrary", "arbitrary"),
                                  vmem_limit_bytes=VMEM_LIMIT)
    idx_hs = max(1, min(IDX_HEADS, DSA_ATTEND_BLOCK // rows))
    score = functools.partial(_dsa_score_kernel, pos0=pos0, s_real=s_real, ts=ts, rows=rows,
                              hs=idx_hs, wi_scale=(IDX_HEADS * IDX_DIM) ** -0.5)
    scores, row_max = pl.pallas_call(
        score,
        grid=(nb, t // tq_att),
        in_specs=[pl.BlockSpec((None, tq_att, qi.shape[2]), lambda b, i: (b, i, 0)),
                  pl.BlockSpec((None, tq_att, wi.shape[2]), lambda b, i: (b, i, 0)),
                  pl.BlockSpec((None, IDX_DIM, s_pad), lambda b, i: (b, 0, 0),
                               pipeline_mode=pl.Buffered(1))],
        out_specs=[pl.BlockSpec((None, tq_att, s_pad), lambda b, i: (b, i, 0)),
                   pl.BlockSpec((None, tq_att, LANES), lambda b, i: (b, i, 0))],
        out_shape=[jax.ShapeDtypeStruct((nb, t, s_pad), F32),
                   jax.ShapeDtypeStruct((nb, t, LANES), F32)],
        scratch_shapes=[pltpu.VMEM((IDX_HEADS // idx_hs, idx_hs * tq_att, IDX_DIM), BF16),
                        pltpu.VMEM((IDX_HEADS, tq_att, LANES), F32)],
        compiler_params=params,
        name="dsa_score",
    )(qi, wi, kit)
    sel = functools.partial(_dsa_select_kernel, pos0=pos0, s_real=s_real, ts=ts, top_k=top_k)
    bias = pl.pallas_call(
        sel,
        grid=(nb, t // tq_sel),
        in_specs=[pl.BlockSpec((None, tq_sel, s_pad), lambda b, i: (b, i, 0)),
                  pl.BlockSpec((None, tq_sel, LANES), lambda b, i: (b, i, 0))],
        out_specs=pl.BlockSpec((None, tq_sel, s_pad), lambda b, i: (b, i, 0)),
        out_shape=jax.ShapeDtypeStruct((nb, t, s_pad), F32),
        compiler_params=params,
        name="dsa_select",
    )(scores, row_max)
    hs = max(1, min(GROUP, DSA_ATTEND_BLOCK // rows))
    att = functools.partial(_dsa_attend_kernel, pos0=pos0, s_real=s_real, ts=ts, rows=rows, hs=hs)
    tile3 = lambda w, dt: pltpu.VMEM((N_HEADS // hs, hs * tq_att, w), dt)
    return pl.pallas_call(
        att,
        grid=(nb, t // tq_att),
        in_specs=[pl.BlockSpec((None, tq_att, nq), lambda b, i: (b, i, 0)),
                  pl.BlockSpec((None, tq_att, s_pad), lambda b, i: (b, i, 0)),
                  pl.BlockSpec((None, N_KV_HEADS, HEAD_DIM, s_pad), lambda b, i: (b, 0, 0, 0),
                               pipeline_mode=pl.Buffered(1)),
                  pl.BlockSpec((None, N_KV_HEADS, s_pad, LANES), lambda b, i: (b, 0, 0, 0),
                               pipeline_mode=pl.Buffered(1))],
        out_specs=pl.BlockSpec((None, tq_att, nq), lambda b, i: (b, i, 0)),
        out_shape=jax.ShapeDtypeStruct((nb, t, nq), F32),
        scratch_shapes=[tile3(HEAD_DIM, BF16), tile3(LANES, F32), tile3(LANES, F32)],
        compiler_params=params,
        name="dsa_attend",
    )(q, bias, kt, v)


def _tok_b_kernel(x1_ref, yc_ref, ya_ref, mk_ref, mv_ref, wmo_ref, g2_ref, b2_ref,
                  wq_ref, wo_ref, g3_ref, b3_ref, wg_ref, wu_ref, wd_ref, g4_ref, b4_ref,
                  y_ref, acc_ref, *, alpha, f_chunk):
    x1 = x1_ref[...]
    mixed = jnp.concatenate([yc_ref[...], ya_ref[...]], axis=-1).astype(BF16)
    x2 = _layer_norm(alpha * x1 + _dot(mixed, wmo_ref[...]), g2_ref[...], b2_ref[...])

    d = x2.shape[1]
    dh = d // MEM_HEADS
    qm = (_dot(x2.astype(BF16), wq_ref[...]) * (dh ** -0.5)).astype(BF16)
    mk = mk_ref[...].astype(BF16)
    mv = mv_ref[...].astype(BF16)
    heads = []
    for h in range(MEM_HEADS):
        sl = slice(h * dh, (h + 1) * dh)
        s = _dot_nt(qm[:, sl], mk[:, sl])
        e = jnp.exp(s - jnp.max(s, axis=-1, keepdims=True))
        p = e / jnp.sum(e, axis=-1, keepdims=True)
        heads.append(_dot(p.astype(BF16), mv[:, sl]))
    o = jnp.concatenate(heads, axis=-1).astype(BF16)
    x3 = _layer_norm(alpha * x2 + _dot(o, wo_ref[...]), g3_ref[...], b3_ref[...])

    _swiglu_into(acc_ref, x3.astype(BF16), wg_ref, wu_ref, wd_ref, f_chunk)
    y_ref[...] = _layer_norm(alpha * x3 + 0.5 * acc_ref[...], g4_ref[...], b4_ref[...])


def _tok_b(x1, yc, ya, mk, mv, wmo, g2, b2, wq, wo, g3, b3, wg, wu, wd, g4, b4,
           *, alpha, tm, seg_len):
    n, d = x1.shape
    n_mem = mk.shape[1]
    tiles_per_seg = seg_len // tm
    row = lambda w: pl.BlockSpec((tm, w), lambda i: (i, 0))
    mem = pl.BlockSpec((None, n_mem, d), lambda i: (i // tiles_per_seg, 0, 0))
    consts = [wmo, g2, b2, wq, wo, g3, b3, wg, wu, wd, g4, b4]
    kern = functools.partial(_tok_b_kernel, alpha=alpha, f_chunk=FFN_CHUNK)
    return pl.pallas_call(
        kern,
        grid=(n // tm,),
        in_specs=[row(d), row(yc.shape[1]), row(ya.shape[1]), mem, mem]
                 + [_const_spec(c.shape) for c in consts],
        out_specs=row(d),
        out_shape=jax.ShapeDtypeStruct((n, d), F32),
        scratch_shapes=[pltpu.VMEM((tm, d), F32)],
        compiler_params=pltpu.CompilerParams(dimension_semantics=("arbitrary",),
                                             vmem_limit_bytes=VMEM_LIMIT),
        name="tok_b",
    )(x1, yc, ya, mk, mv, *consts)


def _round_up(x, m):
    return (x + m - 1) // m * m


def _layer(x, conv_prev, k_past, v_past, ik_past, mem_k, mem_v, p, *, alpha):
    b, t, d = x.shape
    past = k_past.shape[1]
    d_conv = p["conv_w"].shape[1]
    top_k = min(TOPK_MAX, (past + t) // 4)
    tm = min(TOKEN_TILE, t)
    ts = DSA_KEY_TILE

    cprev = jnp.pad(conv_prev, ((0, 0), (6, 0), (0, 0)))
    x1, yconv, q, k, v, qi, tail, ulast = _tok_a(
        x.reshape(b * t, d), cprev, p["ffn1_gate"], p["ffn1_up"], p["ffn1_down"],
        p["ln1_g"], p["ln1_b"], p["w_mix_in"], p["conv_w"], alpha=alpha, tm=tm, seg_len=t)
    new_conv = ulast.reshape(b, t // tm, 8, d_conv)[:, -1, 6:8, :]
    k = k.reshape(b, t, N_KV_HEADS, HEAD_DIM)
    v = v.reshape(b, t, N_KV_HEADS, HEAD_DIM)
    ki = tail[:, :IDX_DIM].reshape(b, t, IDX_DIM)
    wi = tail[:, IDX_DIM:IDX_DIM + IDX_HEADS].reshape(b, t, IDX_HEADS)

    s_real = past + t
    s_pad = _round_up(s_real, ts)
    pad = lambda a: jnp.pad(a, ((0, 0), (0, s_pad - s_real)) + ((0, 0),) * (a.ndim - 2))
    k_all = pad(jnp.concatenate([k_past, k], axis=1)).astype(BF16)
    v_all = pad(jnp.concatenate([v_past, v], axis=1)).astype(BF16)
    ki_all = pad(jnp.concatenate([ik_past, ki], axis=1)).astype(BF16)
    ones_col = (jnp.arange(LANES - HEAD_DIM) == 0).astype(BF16)
    v_aug = jnp.concatenate(
        [v_all.transpose(0, 2, 1, 3),
         jnp.broadcast_to(ones_col, (b, N_KV_HEADS, s_pad, LANES - HEAD_DIM))], axis=-1)
    y_attn = _dsa(q.reshape(b, t, -1), qi.reshape(b, t, -1), wi,
                  ki_all.transpose(0, 2, 1), k_all.transpose(0, 2, 3, 1), v_aug,
                  pos0=past, s_real=s_real, tq_sel=min(DSA_SELECT_ROWS, t),
                  tq_att=min(DSA_ATTEND_ROWS, t), rows=min(DSA_ATTEND_BLOCK, t), ts=ts, top_k=top_k)

    y = _tok_b(x1, yconv, y_attn.reshape(b * t, -1), mem_k.reshape(b, mem_k.shape[1], d),
               mem_v.reshape(b, mem_v.shape[1], d),
               p["w_mix_out"], p["ln2_g"], p["ln2_b"], p["w_mem_q"], p["w_mem_o"],
               p["ln3_g"], p["ln3_b"], p["ffn2_gate"], p["ffn2_up"], p["ffn2_down"],
               p["ln4_g"], p["ln4_b"], alpha=alpha, tm=tm, seg_len=t)
    return y.reshape(b, t, d), new_conv, k, v, ki


def kernel(x_prompt, x_sample, cache_conv, cache_k, cache_v, cache_idx_k, cache_mem_k, cache_mem_v,
           mem_prompt, ffn1_gate, ffn1_up, ffn1_down, ln1_g, ln1_b, w_mix_in, conv_w, w_mix_out,
           ln2_g, ln2_b, w_mem_q, w_mem_k, w_mem_v, w_mem_o, ln3_g, ln3_b,
           ffn2_gate, ffn2_up, ffn2_down, ln4_g, ln4_b):
    depth = ffn1_gate.shape[0]
    alpha = (2.0 * depth) ** 0.25
    b_p, t_p, d = x_prompt.shape
    dt = x_prompt.dtype
    d_conv = conv_w.shape[2]
    n_mem = mem_prompt.shape[1]
    bf = lambda w: w.astype(BF16)
    vec = lambda a: a.reshape(1, -1)

    y_p, y_s = x_prompt, x_sample
    outs_p = [[] for _ in range(6)]
    outs_s = [[] for _ in range(4)]
    for l in range(depth):
        d_in = w_mix_in.shape[2]
        p = dict(ffn1_gate=bf(ffn1_gate[l]), ffn1_up=bf(ffn1_up[l]), ffn1_down=bf(ffn1_down[l]),
                 ln1_g=vec(ln1_g[l]), ln1_b=vec(ln1_b[l]),
                 w_mix_in=jnp.pad(bf(w_mix_in[l]),
                                  ((0, 0), (0, _round_up(d_in, LANES) - d_in))),
                 conv_w=conv_w[l], w_mix_out=bf(w_mix_out[l]),
                 ln2_g=vec(ln2_g[l]), ln2_b=vec(ln2_b[l]),
                 w_mem_q=bf(w_mem_q[l]), w_mem_o=bf(w_mem_o[l]),
                 ln3_g=vec(ln3_g[l]), ln3_b=vec(ln3_b[l]),
                 ffn2_gate=bf(ffn2_gate[l]), ffn2_up=bf(ffn2_up[l]), ffn2_down=bf(ffn2_down[l]),
                 ln4_g=vec(ln4_g[l]), ln4_b=vec(ln4_b[l]))
        mem_kv = _memkv(mem_prompt.reshape(b_p * n_mem, d),
                        jnp.stack([bf(w_mem_k[l]), bf(w_mem_v[l])]))
        mem_k = mem_kv[0].reshape(b_p, n_mem, MEM_HEADS, d // MEM_HEADS)
        mem_v = mem_kv[1].reshape(b_p, n_mem, MEM_HEADS, d // MEM_HEADS)
        y_p, c_new, k_new, v_new, ik_new = _layer(
            y_p, jnp.zeros((b_p, 2, d_conv), dt),
            jnp.zeros((b_p, 0, N_KV_HEADS, HEAD_DIM), dt), jnp.zeros((b_p, 0, N_KV_HEADS, HEAD_DIM), dt),
            jnp.zeros((b_p, 0, IDX_DIM), dt), mem_k, mem_v, p, alpha=alpha)
        for lst, a in zip(outs_p, (c_new, k_new, v_new, ik_new, mem_k, mem_v)):
            lst.append(a)
        y_s, c_new, k_new, v_new, ik_new = _layer(
            y_s, cache_conv[l], cache_k[l], cache_v[l], cache_idx_k[l],
            cache_mem_k[l], cache_mem_v[l], p, alpha=alpha)
        for lst, a in zip(outs_s, (c_new, k_new, v_new, ik_new)):
            lst.append(a)
    return (y_p, y_s, *[jnp.stack(o) for o in outs_p], *[jnp.stack(o) for o in outs_s])
```
